```python
import math
import jax, jax.numpy as jnp
from jax import lax
import numpy as np

D_MODEL = 2048
BATCH = 1
SEQ = 8192
DEPTH = 2
DEC_BATCH = 32
DEC_SEQ = 4
PAST_LEN = 8192
PAGE_SIZE = 128

EPS = 1e-6
A_WIDTH = D_MODEL // 2
A_HEADS = 8
A_HEAD_DIM = A_WIDTH // A_HEADS
A_CHUNK = 128
B_HEADS = 4
B_QK_DIM = D_MODEL // 2 // B_HEADS
B_V_DIM = D_MODEL // 2 // B_HEADS
B_CHUNK = 128
B_ROPE_THETA = 10000.0
IN0_WIDTH = 2 * A_WIDTH + 2 * B_HEADS * B_QK_DIM + 2 * B_HEADS * B_V_DIM
MIX0_WIDTH = A_WIDTH + B_HEADS * B_V_DIM
C_HEADS = 16
C_HEAD_DIM = D_MODEL // (2 * C_HEADS)
C_V_DIM = 2 * C_HEAD_DIM
C_ROT_DIM = C_HEAD_DIM // 4
C_ROPE_THETA = 500000.0
C_Q_BLOCK = 128
C_SCALE = C_HEAD_DIM ** -0.5
IN1_WIDTH = 3 * C_HEADS * C_V_DIM
MIX1_WIDTH = C_HEADS * C_V_DIM
N_EXPERTS = 16
N_GROUPS = 4
EXPERTS_PER_GROUP = N_EXPERTS // N_GROUPS
GROUP_SCORE_K = 2
TOP_K = 2
EXPERT_FF = 1536
MOE_BLOCK = 128

kernel_name = 'hybrid_gmlp_retention_diffattn_moe_step'


def rms_norm(x, g):
    xf = x.astype(jnp.float32)
    y = xf * lax.rsqrt(jnp.mean(xf * xf, axis=-1, keepdims=True) + EPS)
    return (y * g.astype(jnp.float32)).astype(x.dtype)


def layer_norm(x, g, b):
    xf = x.astype(jnp.float32)
    mu = jnp.mean(xf, axis=-1, keepdims=True)
    xc = xf - mu
    y = xc * lax.rsqrt(jnp.mean(xc * xc, axis=-1, keepdims=True) + EPS)
    return (y * g.astype(jnp.float32) + b.astype(jnp.float32)).astype(x.dtype)


def rope(x, pos, rot_dim, theta):
    half = rot_dim // 2
    freqs = theta ** (-jnp.arange(half, dtype=jnp.float32) / half)
    ang = pos.astype(jnp.float32)[:, None] * freqs[None, :]
    shape = (pos.shape[0],) + (1,) * (x.ndim - 3) + (half,)
    cos = jnp.cos(ang).reshape(shape)
    sin = jnp.sin(ang).reshape(shape)
    xf = x.astype(jnp.float32)
    x1 = xf[..., :half]
    x2 = xf[..., half:rot_dim]
    out = jnp.concatenate([x1 * cos - x2 * sin, x2 * cos + x1 * sin, xf[..., rot_dim:]], axis=-1)
    return out.astype(x.dtype)


def ada_mod(c, w, b, dtype):
    m = jax.nn.silu(c.astype(jnp.float32)) @ w.astype(jnp.float32) + b.astype(jnp.float32)
    m = m.reshape(c.shape[0], 6, D_MODEL).astype(dtype)
    return [m[:, i, None, :] for i in range(6)]


def retention_chunk(state, qkv):
    q, k, v = (t.astype(jnp.float32) for t in qkv)
    L = q.shape[1]
    log_g = jnp.log(1.0 - 2.0 ** (-5.0 - jnp.arange(B_HEADS, dtype=jnp.float32)))
    i = jnp.arange(L, dtype=jnp.float32)
    rel = i[:, None] - i[None, :]
    decay = jnp.where(rel[None] >= 0, jnp.exp(jnp.maximum(rel, 0.0)[None] * log_g[:, None, None]), 0.0)
    inner = jnp.einsum('bihd,bjhd->bhij', q, k) * decay[None]
    o = jnp.einsum('bhij,bjhe->bihe', inner, v)
    cross_decay = jnp.exp((i + 1.0)[:, None] * log_g[None, :])
    o = o + jnp.einsum('bihd,bhde->bihe', q, state) * cross_decay[None, :, :, None]
    k_decay = jnp.exp((L - 1.0 - i)[:, None] * log_g[None, :])
    new_state = jnp.exp(L * log_g)[None, :, None, None] * state + jnp.einsum('bjhd,bjhe->bhde', k * k_decay[None, :, :, None], v)
    return new_state, o


def mixer0(h, pos, ret_state, w_in, ln_g, ln_b, w_s, b_s, ret_g, w_out):
    B, S, _ = h.shape
    p = h @ w_in
    bq = B_HEADS * B_QK_DIM
    bv = B_HEADS * B_V_DIM
    u, v, q, k, vr, g = jnp.split(p, [A_WIDTH, 2 * A_WIDTH, 2 * A_WIDTH + bq, 2 * A_WIDTH + 2 * bq, 2 * A_WIDTH + 2 * bq + bv], axis=-1)
    u = jax.nn.gelu(u)
    v = layer_norm(jax.nn.gelu(v), ln_g, ln_b)
    La = min(S, A_CHUNK)
    vc = v.reshape(B, S // La, La, A_HEADS, A_HEAD_DIM)
    w_mask = jnp.where(jnp.tril(jnp.ones((La, La), dtype=bool))[None], w_s[:, :La, :La], 0.0)
    sv = jnp.einsum('hts,bcshd->bcthd', w_mask.astype(v.dtype), vc) + b_s[:, :La].T[None, None, :, :, None].astype(v.dtype)
    out_a = u * sv.reshape(B, S, A_WIDTH)
    q = rope(q.reshape(B, S, B_HEADS, B_QK_DIM), pos, B_QK_DIM, B_ROPE_THETA)
    k = rope(k.reshape(B, S, B_HEADS, B_QK_DIM), pos, B_QK_DIM, B_ROPE_THETA) * (B_QK_DIM ** -0.5)
    vr = vr.reshape(B, S, B_HEADS, B_V_DIM)
    Lb = min(S, B_CHUNK)
    nc = S // Lb
    to_chunks = lambda t: jnp.moveaxis(t.reshape((B, nc, Lb) + t.shape[2:]), 1, 0)
    new_state, o = lax.scan(retention_chunk, ret_state.astype(jnp.float32), (to_chunks(q), to_chunks(k), to_chunks(vr)))
    o = jnp.moveaxis(o, 0, 1).reshape(B, S, B_HEADS, B_V_DIM)
    o = rms_norm(o, ret_g.reshape(B_HEADS, B_V_DIM)).reshape(B, S, bv).astype(h.dtype)
    out_b = jax.nn.silu(g) * o
    y = jnp.concatenate([out_a, out_b], axis=-1) @ w_out
    return y, new_state.astype(h.dtype), v


def diff_qkv(h, pos, w_in, q_g, k_g):
    B, S, _ = h.shape
    q, k, v = jnp.split(h @ w_in, 3, axis=-1)
    q = rope(rms_norm(q.reshape(B, S, C_HEADS, 2, C_HEAD_DIM), q_g), pos, C_ROT_DIM, C_ROPE_THETA)
    k = rope(rms_norm(k.reshape(B, S, C_HEADS, 2, C_HEAD_DIM), k_g), pos, C_ROT_DIM, C_ROPE_THETA)
    return q, k, v.reshape(B, S, C_HEADS, C_V_DIM)


def diff_lambda(lq1, lk1, lq2, lk2, lambda_init):
    f = lambda t: t.astype(jnp.float32)
    return jnp.exp(jnp.sum(f(lq1) * f(lk1))) - jnp.exp(jnp.sum(f(lq2) * f(lk2))) + lambda_init


def diff_merge(o, lambda_init, subln_g, w_out):
    B, S = o.shape[:2]
    o = rms_norm(o, subln_g) * (1.0 - lambda_init)
    return o.reshape(B, S, MIX1_WIDTH).astype(w_out.dtype) @ w_out


def diff_prompt(h, pos, lam, lambda_init, w_in, q_g, k_g, subln_g, w_out):
    B, S, _ = h.shape
    q, k, v = diff_qkv(h, pos, w_in, q_g, k_g)
    kpos = jnp.arange(S)

    def attend_block(i):
        q_blk = lax.dynamic_slice_in_dim(q, i * C_Q_BLOCK, C_Q_BLOCK, axis=1)
        qpos = i * C_Q_BLOCK + jnp.arange(C_Q_BLOCK)
        s = jnp.einsum('bqhrd,bkhrd->bhrqk', q_blk, k, preferred_element_type=jnp.float32) * C_SCALE
        s = jnp.where(kpos[None, :] <= qpos[:, None], s, -jnp.inf)
        p = jax.nn.softmax(s, axis=-1)
        a = p[:, :, 0] - lam * p[:, :, 1]
        return jnp.einsum('bhqk,bkhe->bqhe', a.astype(v.dtype), v)

    o = lax.map(attend_block, jnp.arange(S // C_Q_BLOCK))
    o = jnp.moveaxis(o, 0, 1).reshape(B, S, C_HEADS, C_V_DIM)
    y = diff_merge(o, lambda_init, subln_g, w_out)
    return y, k.reshape(B, S, C_HEADS, 2 * C_HEAD_DIM), v


def diff_sample(h, pos, cache_k_l, cache_v_l, page_table, lam, lambda_init, w_in, q_g, k_g, subln_g, w_out):
    B, S, _ = h.shape
    q, k, v = diff_qkv(h, pos, w_in, q_g, k_g)
    n_past = page_table.shape[1] * PAGE_SIZE
    k_past = cache_k_l[page_table].reshape(B, n_past, C_HEADS, 2, C_HEAD_DIM)
    v_past = cache_v_l[page_table].reshape(B, n_past, C_HEADS, C_V_DIM)
    s_past = jnp.einsum('bqhrd,bkhrd->bhrqk', q, k_past.astype(q.dtype), preferred_element_type=jnp.float32) * C_SCALE
    s_new = jnp.einsum('bqhrd,bkhrd->bhrqk', q, k, preferred_element_type=jnp.float32) * C_SCALE
    s_new = jnp.where(jnp.tril(jnp.ones((S, S), dtype=bool)), s_new, -jnp.inf)
    p = jax.nn.softmax(jnp.concatenate([s_past, s_new], axis=-1), axis=-1)
    a = (p[:, :, 0] - lam * p[:, :, 1]).astype(v.dtype)
    o = jnp.einsum('bhqk,bkhe->bqhe', a[..., :n_past], v_past.astype(v.dtype)) + jnp.einsum('bhqk,bkhe->bqhe', a[..., n_past:], v)
    y = diff_merge(o, lambda_init, subln_g, w_out)
    return y, k.reshape(B, S, C_HEADS, 2 * C_HEAD_DIM), v


def moe(h, w_router, router_bias, w_gate_l, w_up_l, w_down_l):
    T = h.shape[0]
    scores = jax.nn.sigmoid(h.astype(jnp.float32) @ w_router.astype(jnp.float32))
    sel = scores + router_bias.astype(jnp.float32)
    grp = sel.reshape(T, N_GROUPS, EXPERTS_PER_GROUP)
    grp_score = jnp.sum(lax.top_k(grp, GROUP_SCORE_K)[0], axis=-1)
    g_idx = jnp.argmax(grp_score, axis=-1)
    in_grp = jnp.take_along_axis(grp, g_idx[:, None, None], axis=1)[:, 0]
    _, local = lax.top_k(in_grp, TOP_K)
    expert_idx = g_idx[:, None] * EXPERTS_PER_GROUP + local
    gate = jnp.take_along_axis(scores, expert_idx, axis=1)
    gate = gate / jnp.sum(gate, axis=-1, keepdims=True)
    A = T * TOP_K
    flat_e = expert_idx.reshape(A).astype(jnp.int32)
    order = jnp.argsort(flat_e)
    sorted_e = flat_e[order]
    counts = jax.ops.segment_sum(jnp.ones((A,), jnp.int32), flat_e, num_segments=N_EXPERTS)
    starts = jnp.cumsum(counts) - counts
    padded = (counts + MOE_BLOCK - 1) // MOE_BLOCK * MOE_BLOCK
    pends = jnp.cumsum(padded)
    pstarts = pends - padded
    dest_sorted = pstarts[sorted_e] + (jnp.arange(A, dtype=jnp.int32) - starts[sorted_e])
    n_blocks = -(-A // MOE_BLOCK) + N_EXPERTS
    n_slots = n_blocks * MOE_BLOCK
    slot_token = jnp.zeros((n_slots,), jnp.int32).at[dest_sorted].set((order // TOP_K).astype(jnp.int32))
    block_expert = jnp.clip(jnp.searchsorted(pends, jnp.arange(n_blocks, dtype=jnp.int32) * MOE_BLOCK, side='right'), 0, N_EXPERTS - 1)
    x_slots = h[slot_token].reshape(n_blocks, MOE_BLOCK, D_MODEL)

    def expert_block(args):
        xb, e = args
        return (jax.nn.silu(xb @ w_gate_l[e]) * (xb @ w_up_l[e])) @ w_down_l[e]

    y_slots = lax.map(expert_block, (x_slots, block_expert)).reshape(n_slots, D_MODEL)
    dest = jnp.zeros((A,), jnp.int32).at[order].set(dest_sorted).reshape(T, TOP_K)
    return jnp.einsum('tk,tkd->td', gate.astype(h.dtype), y_slots[dest])


def setup_inputs(seed: int = 0) -> dict:
    key = jax.random.key(seed)
    ks = jax.random.split(key, 40)
    f32 = jnp.float32
    nrm = lambda i, shape, scale: jax.random.normal(ks[i], shape, f32) * scale
    n_even = (DEPTH + 1) // 2
    n_odd = DEPTH // 2
    n_pages = PAST_LEN // PAGE_SIZE
    n_used = DEC_BATCH * n_pages
    n_pool = n_used + (n_used + 3) // 4
    page_table = jax.random.permutation(ks[7], n_pool)[:n_used].reshape(DEC_BATCH, n_pages).astype(jnp.int32)
    return {
        'x_prompt': nrm(0, (BATCH, SEQ, D_MODEL), 1.0),
        'x_sample': nrm(1, (DEC_BATCH, DEC_SEQ, D_MODEL), 1.0),
        'c_prompt': nrm(2, (BATCH, D_MODEL), 1.0),
        'c_sample': nrm(3, (DEC_BATCH, D_MODEL), 1.0),
        'state_ret': nrm(4, (n_even, DEC_BATCH, B_HEADS, B_QK_DIM, B_V_DIM), 0.1),
        'cache_k': nrm(5, (n_odd, n_pool, PAGE_SIZE, C_HEADS, 2 * C_HEAD_DIM), 1.0),
        'cache_v': nrm(6, (n_odd, n_pool, PAGE_SIZE, C_HEADS, C_V_DIM), 1.0),
        'page_table': page_table,
        'w_ada': nrm(8, (DEPTH, D_MODEL, 6 * D_MODEL), 0.5 * D_MODEL ** -0.5),
        'b_ada': nrm(9, (DEPTH, 6 * D_MODEL), 0.02),
        'norm_mix': 1.0 + nrm(10, (DEPTH, D_MODEL), 0.02),
        'norm_ffn': 1.0 + nrm(11, (DEPTH, D_MODEL), 0.02),
        'w_in0': nrm(12, (n_even, D_MODEL, IN0_WIDTH), D_MODEL ** -0.5),
        'gmlp_ln_g': 1.0 + nrm(13, (n_even, A_WIDTH), 0.02),
        'gmlp_ln_b': nrm(14, (n_even, A_WIDTH), 0.02),
        'gmlp_w_s': nrm(15, (n_even, A_HEADS, A_CHUNK, A_CHUNK), A_CHUNK ** -0.5),
        'gmlp_b_s': 1.0 + nrm(16, (n_even, A_HEADS, A_CHUNK), 0.02),
        'ret_norm_g': 1.0 + nrm(17, (n_even, B_HEADS * B_V_DIM), 0.02),
        'w_out0': nrm(18, (n_even, MIX0_WIDTH, D_MODEL), MIX0_WIDTH ** -0.5),
        'w_in1': nrm(19, (n_odd, D_MODEL, IN1_WIDTH), D_MODEL ** -0.5),
        'q_norm_g': 1.0 + nrm(20, (n_odd, C_HEAD_DIM), 0.02),
        'k_norm_g': 1.0 + nrm(21, (n_odd, C_HEAD_DIM), 0.02),
        'lambda_q1': nrm(22, (n_odd, C_HEAD_DIM), 0.1),
        'lambda_k1': nrm(23, (n_odd, C_HEAD_DIM), 0.1),
        'lambda_q2': nrm(24, (n_odd, C_HEAD_DIM), 0.1),
        'lambda_k2': nrm(25, (n_odd, C_HEAD_DIM), 0.1),
        'subln_g': 1.0 + nrm(26, (n_odd, C_V_DIM), 0.02),
        'w_out1': nrm(27, (n_odd, MIX1_WIDTH, D_MODEL), MIX1_WIDTH ** -0.5),
        'w_router': nrm(28, (D_MODEL, N_EXPERTS), D_MODEL ** -0.5),
        'router_bias': nrm(29, (N_EXPERTS,), 0.01),
        'w_gate': nrm(30, (DEPTH, N_EXPERTS, D_MODEL, EXPERT_FF), D_MODEL ** -0.5),
        'w_up': nrm(31, (DEPTH, N_EXPERTS, D_MODEL, EXPERT_FF), D_MODEL ** -0.5),
        'w_down': nrm(32, (DEPTH, N_EXPERTS, EXPERT_FF, D_MODEL), EXPERT_FF ** -0.5),
    }


def reference(x_prompt, x_sample, c_prompt, c_sample, state_ret, cache_k, cache_v, page_table,
              w_ada, b_ada, norm_mix, norm_ffn,
              w_in0, gmlp_ln_g, gmlp_ln_b, gmlp_w_s, gmlp_b_s, ret_norm_g, w_out0,
              w_in1, q_norm_g, k_norm_g, lambda_q1, lambda_k1, lambda_q2, lambda_k2, subln_g, w_out1,
              w_router, router_bias, w_gate, w_up, w_down):
    pos_p = jnp.arange(SEQ)
    pos_s = PAST_LEN + jnp.arange(DEC_SEQ)
    xp, xs = x_prompt, x_sample
    ret_p_l, ret_s_l, gv_s_l = [], [], []
    kp_l, vp_l, ks_l, vs_l = [], [], [], []
    for l in range(DEPTH):
        j = l // 2
        mp = ada_mod(c_prompt, w_ada[l], b_ada[l], xp.dtype)
        ms = ada_mod(c_sample, w_ada[l], b_ada[l], xs.dtype)
        hp = rms_norm(xp, norm_mix[l]) * (1 + mp[1]) + mp[0]
        hs = rms_norm(xs, norm_mix[l]) * (1 + ms[1]) + ms[0]
        if l % 2 == 0:
            params0 = (w_in0[j], gmlp_ln_g[j], gmlp_ln_b[j], gmlp_w_s[j], gmlp_b_s[j], ret_norm_g[j], w_out0[j])
            zero_state = jnp.zeros((BATCH, B_HEADS, B_QK_DIM, B_V_DIM), jnp.float32)
            yp, rp, _ = mixer0(hp, pos_p, zero_state, *params0)
            ys, rs, gvs = mixer0(hs, pos_s, state_ret[j], *params0)
            ret_p_l.append(rp)
            ret_s_l.append(rs)
            gv_s_l.append(gvs)
        else:
            lambda_init = 0.8 - 0.6 * math.exp(-0.3 * l)
            lam = diff_lambda(lambda_q1[j], lambda_k1[j], lambda_q2[j], lambda_k2[j], lambda_init)
            params1 = (w_in1[j], q_norm_g[j], k_norm_g[j], subln_g[j], w_out1[j])
            yp, kp, vp = diff_prompt(hp, pos_p, lam, lambda_init, *params1)
            ys, ks, vs = diff_sample(hs, pos_s, cache_k[j], cache_v[j], page_table, lam, lambda_init, *params1)
            kp_l.append(kp)
            vp_l.append(vp)
            ks_l.append(ks)
            vs_l.append(vs)
        xp = xp + mp[2] * yp
        xs = xs + ms[2] * ys
        hp = rms_norm(xp, norm_ffn[l]) * (1 + mp[4]) + mp[3]
        hs = rms_norm(xs, norm_ffn[l]) * (1 + ms[4]) + ms[3]
        xp = xp + mp[5] * moe(hp.reshape(-1, D_MODEL), w_router, router_bias, w_gate[l], w_up[l], w_down[l]).reshape(xp.shape)
        xs = xs + ms[5] * moe(hs.reshape(-1, D_MODEL), w_router, router_bias, w_gate[l], w_up[l], w_down[l]).reshape(xs.shape)
    return (xp, xs, jnp.stack(ret_p_l), jnp.stack(ret_s_l), jnp.stack(gv_s_l),
            jnp.stack(kp_l), jnp.stack(vp_l), jnp.stack(ks_l), jnp.stack(vs_l))
```

```python
import functools
import math

import jax
import jax.numpy as jnp
from jax import lax
from jax.experimental import pallas as pl
from jax.experimental.pallas import tpu as pltpu

F32 = jnp.float32
BF16 = jnp.bfloat16
I32 = jnp.int32
HIGHEST = lax.Precision.HIGHEST

D = 2048
SEQ = 8192
DEC_B = 32
DEC_S = 4
NS = DEC_B * DEC_S
PAST = 8192
PAGE = 128
EPS = 1e-6
A_W = 1024
A_H = 8
B_H = 4
B_DK = 256
C_H = 16
C_HD = 64
C_VD = 128
C_ROT = 16
N_E = 16
N_G = 4
FF = 1536

LANES = 128
SUBLANES = 8
VMEM_LIMIT_BYTES = 56 * 1024 * 1024

MOE_TM = 256
MOE_FF_TN = 768
MOE_D_TN = 1024
ATT_TQ = 256


def _cparams(sem):
    return pltpu.CompilerParams(dimension_semantics=sem, vmem_limit_bytes=VMEM_LIMIT_BYTES)


def _sigmoid(x):
    return 1.0 / (1.0 + jnp.exp(-x))


def _gelu(x):
    return x * (0.5 * (1.0 + jnp.tanh(0.7978845608028654 * (x + 0.044715 * (x * x * x)))))


def _rms_mod(x, g, sc, sh):
    ms = jnp.mean(x * x, axis=-1, keepdims=True)
    return (x * lax.rsqrt(ms + EPS)) * g * (1.0 + sc) + sh


def _dot(a, b):
    return jnp.dot(a, b, preferred_element_type=F32)


def _dot_nt(a, b):
    return lax.dot_general(a, b, (((1,), (1,)), ((), ())), preferred_element_type=F32)


def _dot_tn(a, b):
    return lax.dot_general(a, b, (((0,), (0,)), ((), ())), preferred_element_type=F32)


def _ada_kernel(c_ref, w_ref, b_ref, o_ref):
    c = c_ref[...]
    s = c * _sigmoid(c)
    o_ref[0] = jnp.dot(s, w_ref[0], preferred_element_type=F32, precision=HIGHEST) + b_ref[0]


def _ada_call(c_all, w_ada, b_ada):
    nl, _, n = w_ada.shape
    mc = c_all.shape[0]
    tn = 1024
    return pl.pallas_call(
        _ada_kernel,
        grid=(nl, n // tn),
        in_specs=[
            pl.BlockSpec((mc, D), lambda l, j: (0, 0)),
            pl.BlockSpec((1, D, tn), lambda l, j: (l, 0, j)),
            pl.BlockSpec((1, 1, tn), lambda l, j: (l, 0, j)),
        ],
        out_specs=pl.BlockSpec((1, mc, tn), lambda l, j: (l, 0, j)),
        out_shape=jax.ShapeDtypeStruct((nl, mc, n), F32),
        compiler_params=_cparams(("parallel", "parallel")),
        name="ada",
    )(c_all, w_ada, b_ada.reshape(nl, 1, n))


def _inproj_kernel(*refs, mode, scale, tn):
    x_ref, g_ref, sc_ref, sh_ref, w_ref = refs[:5]
    h_scr = refs[-1]

    @pl.when(pl.program_id(1) == 0)
    def _():
        h_scr[...] = _rms_mod(x_ref[...], g_ref[...], sc_ref[...], sh_ref[...]).astype(BF16)

    acc = _dot(h_scr[...], w_ref[...])
    if mode in ("q", "k"):
        gq_ref, cos_ref, sa_ref, sb_ref, bd_ref = refs[5:10]
        outs = refs[10:-1]
        gq = gq_ref[...]
        cos_t, sin_a, sin_b, bd = cos_ref[...], sa_ref[...], sb_ref[...], bd_ref[...]
        for j in range(tn // LANES):
            sl = slice(j * LANES, (j + 1) * LANES)
            y = acc[:, sl]
            sq = y * y
            hi = sq.astype(BF16)
            lo = (sq - hi.astype(F32)).astype(BF16)
            ss = _dot(hi, bd) + _dot(lo, bd)
            yn = y * lax.rsqrt(ss * (1.0 / C_HD) + EPS) * gq
            r = yn * cos_t + pltpu.roll(yn, LANES - C_ROT // 2, 1) * sin_a + pltpu.roll(yn, C_ROT // 2, 1) * sin_b
            if scale != 1.0:
                r = r * scale
            if mode == "q":
                outs[0][:, sl] = r.astype(BF16)
            else:
                outs[0][:, sl] = r
                outs[1][:, sl] = r.astype(BF16)
    elif mode == "dual":
        outs = refs[5:-1]
        outs[0][...] = acc
        outs[1][...] = acc.astype(BF16)
    else:
        refs[5][...] = acc


def _inproj_call(x, g, sc, sh, w, *, mode, tm, tn, extras=(), scale=1.0, name="inproj"):
    m = x.shape[0]
    n = w.shape[1]

    def rows(a):
        if a.shape[0] == 1:
            return pl.BlockSpec((1, D), lambda i, j: (0, 0))
        return pl.BlockSpec((tm, D), lambda i, j: (i, 0))

    in_specs = [
        pl.BlockSpec((tm, D), lambda i, j: (i, 0)),
        pl.BlockSpec((1, D), lambda i, j: (0, 0)),
        rows(sc),
        rows(sh),
        pl.BlockSpec((D, tn), lambda i, j: (0, j)),
    ]
    args = [x, g, sc, sh, w]
    o_spec = pl.BlockSpec((tm, tn), lambda i, j: (i, j))
    if mode in ("q", "k"):
        gq, cos_t, sin_a, sin_b, bd = extras
        in_specs += [
            pl.BlockSpec((1, LANES), lambda i, j: (0, 0)),
            pl.BlockSpec((tm, LANES), lambda i, j: (i, 0)),
            pl.BlockSpec((tm, LANES), lambda i, j: (i, 0)),
            pl.BlockSpec((tm, LANES), lambda i, j: (i, 0)),
            pl.BlockSpec((LANES, LANES), lambda i, j: (0, 0)),
        ]
        args += [gq, cos_t, sin_a, sin_b, bd]
    if mode == "q":
        out_shape = [jax.ShapeDtypeStruct((m, n), BF16)]
    elif mode in ("k", "dual"):
        out_shape = [jax.ShapeDtypeStruct((m, n), F32), jax.ShapeDtypeStruct((m, n), BF16)]
    else:
        out_shape = [jax.ShapeDtypeStruct((m, n), F32)]
    res = pl.pallas_call(
        functools.partial(_inproj_kernel, mode=mode, scale=scale, tn=tn),
        grid=(m // tm, n // tn),
        in_specs=in_specs,
        out_specs=[o_spec] * len(out_shape),
        out_shape=out_shape,
        scratch_shapes=[pltpu.VMEM((tm, D), BF16)],
        compiler_params=_cparams(("parallel", "arbitrary")),
        name=name,
    )(*args)
    return res


def _mix0p_kernel(sdec_ref, p_ref, lng_ref, lnb_ref, wm_ref, bs_ref, cos_ref, sin_ref, dec_ref,
                  cd_ref, kd_ref, rg_ref, mix_ref, so_ref, st_scr):
    c = pl.program_id(0)

    @pl.when(c == 0)
    def _():
        st_scr[...] = jnp.zeros_like(st_scr)

    u = _gelu(p_ref[:, 0:A_W])
    v = _gelu(p_ref[:, A_W:2 * A_W])
    mu = jnp.mean(v, axis=-1, keepdims=True)
    vc = v - mu
    v = vc * lax.rsqrt(jnp.mean(vc * vc, axis=-1, keepdims=True) + EPS) * lng_ref[...] + lnb_ref[...]
    vb = v.astype(BF16)
    for hh in range(A_H):
        sl = slice(hh * LANES, (hh + 1) * LANES)
        sv = _dot(wm_ref[hh], vb[:, sl]) + bs_ref[:, sl]
        mix_ref[:, sl] = (u[:, sl] * sv).astype(BF16)

    cos, sin = cos_ref[...], sin_ref[...]
    half = B_DK // 2

    def rope(t):
        t1, t2 = t[:, :half], t[:, half:]
        return jnp.concatenate([t1 * cos - t2 * sin, t2 * cos + t1 * sin], axis=1)

    base = 2 * A_W
    hw = B_H * B_DK
    for hd in range(B_H):
        sl = slice(hd * B_DK, (hd + 1) * B_DK)
        q = rope(p_ref[:, base + hd * B_DK: base + (hd + 1) * B_DK])
        k = rope(p_ref[:, base + hw + hd * B_DK: base + hw + (hd + 1) * B_DK]) * (B_DK ** -0.5)
        vr = p_ref[:, base + 2 * hw + hd * B_DK: base + 2 * hw + (hd + 1) * B_DK]
        g = p_ref[:, base + 3 * hw + hd * B_DK: base + 3 * hw + (hd + 1) * B_DK]
        qb, kb, vrb = q.astype(BF16), k.astype(BF16), vr.astype(BF16)
        inner = _dot_nt(qb, kb) * dec_ref[hd]
        st = st_scr[hd]
        o = _dot(inner.astype(BF16), vrb) + _dot(qb, st.astype(BF16)) * cd_ref[:, sl]
        kdb = (k * kd_ref[:, sl]).astype(BF16)
        st_scr[hd] = sdec_ref[hd] * st + _dot_tn(kdb, vrb)
        on = o * lax.rsqrt(jnp.mean(o * o, axis=-1, keepdims=True) + EPS) * rg_ref[:, sl]
        mix_ref[:, A_W + hd * B_DK: A_W + (hd + 1) * B_DK] = ((g * _sigmoid(g)) * on).astype(BF16)

    @pl.when(c == pl.num_programs(0) - 1)
    def _():
        so_ref[...] = st_scr[...]


def _mix0p_call(p, lng, lnb, wm, bs, cos, sin, dec, cd, kd, rg, sdec):
    m = p.shape[0]
    ch = 128
    full = lambda shape: pl.BlockSpec(shape, lambda c: (0,) * len(shape))
    return pl.pallas_call(
        _mix0p_kernel,
        grid=(m // ch,),
        in_specs=[
            pl.BlockSpec(memory_space=pltpu.SMEM),
            pl.BlockSpec((ch, p.shape[1]), lambda c: (c, 0)),
            full((1, A_W)), full((1, A_W)),
            full((A_H, ch, ch)), full((ch, A_W)),
            pl.BlockSpec((ch, LANES), lambda c: (c, 0)),
            pl.BlockSpec((ch, LANES), lambda c: (c, 0)),
            full((B_H, ch, ch)), full((ch, B_H * B_DK)), full((ch, B_H * B_DK)),
            full((1, B_H * B_DK)),
        ],
        out_specs=[
            pl.BlockSpec((ch, D), lambda c: (c, 0)),
            full((B_H, B_DK, B_DK)),
        ],
        out_shape=[jax.ShapeDtypeStruct((m, D), BF16), jax.ShapeDtypeStruct((B_H, B_DK, B_DK), F32)],
        scratch_shapes=[pltpu.VMEM((B_H, B_DK, B_DK), F32)],
        compiler_params=_cparams(("arbitrary",)),
        name="mix0_prompt",
    )(sdec, p, lng, lnb, wm, bs, cos, sin, dec, cd, kd, rg)


def _mix0s_kernel(sdec_ref, p_ref, lng_ref, lnb_ref, cw_ref, bs_ref, cos_ref, sin_ref, dm_ref,
                  cd_ref, kd_ref, rg_ref, st_ref, mix_ref, gv_ref, so_ref):
    rows = 2 * DEC_S

    def shift_rows(t, d):
        return t if d == 0 else pltpu.roll(t, d, 0)

    u = _gelu(p_ref[:, 0:A_W])
    v = _gelu(p_ref[:, A_W:2 * A_W])
    mu = jnp.mean(v, axis=-1, keepdims=True)
    vc = v - mu
    v = vc * lax.rsqrt(jnp.mean(vc * vc, axis=-1, keepdims=True) + EPS) * lng_ref[...] + lnb_ref[...]
    gv_ref[...] = v
    sv = bs_ref[...]
    for d in range(DEC_S):
        sv = sv + cw_ref[d] * shift_rows(v, d)
    mix_ref[:, 0:A_W] = u * sv

    cos, sin = cos_ref[...], sin_ref[...]
    half = B_DK // 2

    def rope(t):
        t1, t2 = t[:, :half], t[:, half:]
        return jnp.concatenate([t1 * cos - t2 * sin, t2 * cos + t1 * sin], axis=1)

    grp = lax.shift_right_logical(lax.broadcasted_iota(I32, (rows, B_DK), 0), 2)
    base = 2 * A_W
    hw = B_H * B_DK
    for hd in range(B_H):
        sl = slice(hd * B_DK, (hd + 1) * B_DK)
        q = rope(p_ref[:, base + hd * B_DK: base + (hd + 1) * B_DK])
        k = rope(p_ref[:, base + hw + hd * B_DK: base + hw + (hd + 1) * B_DK]) * (B_DK ** -0.5)
        vr = p_ref[:, base + 2 * hw + hd * B_DK: base + 2 * hw + (hd + 1) * B_DK]
        g = p_ref[:, base + 3 * hw + hd * B_DK: base + 3 * hw + (hd + 1) * B_DK]
        o = jnp.zeros((rows, B_DK), F32)
        for d in range(DEC_S):
            s = jnp.sum(q * shift_rows(k, d), axis=-1, keepdims=True)
            o = o + (s * dm_ref[d][:, sl]) * shift_rows(vr, d)
        kd = k * kd_ref[:, sl]
        for bi in range(2):
            st = st_ref[bi, hd]
            mine = grp == bi
            o = o + jnp.where(mine, _dot(q, st) * cd_ref[:, sl], 0.0)
            so_ref[bi, hd] = sdec_ref[hd] * st + _dot_tn(jnp.where(mine, kd, 0.0), vr)
        on = o * lax.rsqrt(jnp.mean(o * o, axis=-1, keepdims=True) + EPS) * rg_ref[:, sl]
        mix_ref[:, A_W + hd * B_DK: A_W + (hd + 1) * B_DK] = (g * _sigmoid(g)) * on


def _mix0s_call(p, st_in, lng, lnb, cw, bs, cos, sin, dm, cd, kd, rg, sdec):
    rows = 2 * DEC_S
    nb = st_in.shape[0]
    full = lambda shape: pl.BlockSpec(shape, lambda c: (0,) * len(shape))
    return pl.pallas_call(
        _mix0s_kernel,
        grid=(nb // 2,),
        in_specs=[
            pl.BlockSpec(memory_space=pltpu.SMEM),
            pl.BlockSpec((rows, p.shape[1]), lambda c: (c, 0)),
            full((1, A_W)), full((1, A_W)),
            full((DEC_S, rows, A_W)), full((rows, A_W)),
            full((rows, LANES)), full((rows, LANES)),
            full((DEC_S, rows, B_H * B_DK)), full((rows, B_H * B_DK)), full((rows, B_H * B_DK)),
            full((1, B_H * B_DK)),
            pl.BlockSpec((2, B_H, B_DK, B_DK), lambda c: (c, 0, 0, 0)),
        ],
        out_specs=[
            pl.BlockSpec((rows, D), lambda c: (c, 0)),
            pl.BlockSpec((rows, A_W), lambda c: (c, 0)),
            pl.BlockSpec((2, B_H, B_DK, B_DK), lambda c: (c, 0, 0, 0)),
        ],
        out_shape=[
            jax.ShapeDtypeStruct((nb * DEC_S, D), F32),
            jax.ShapeDtypeStruct((nb * DEC_S, A_W), F32),
            jax.ShapeDtypeStruct(st_in.shape, F32),
        ],
        compiler_params=_cparams(("parallel",)),
        name="mix0_sample",
    )(sdec, p, lng, lnb, cw, bs, cos, sin, dm, cd, kd, rg, st_in)


def _outproj_kernel(mix_ref, w_ref, x_ref, gt_ref, g2_ref, sc_ref, sh_ref, wr_ref, xo_ref, h_ref, lg_ref):
    y = _dot(mix_ref[...].astype(BF16), w_ref[...])
    xn = x_ref[...] + gt_ref[...] * y
    xo_ref[...] = xn
    h = _rms_mod(xn, g2_ref[...], sc_ref[...], sh_ref[...])
    h_ref[...] = h
    lg_ref[...] = jnp.dot(h, wr_ref[...], preferred_element_type=F32, precision=HIGHEST)


def _outproj_call(mix, w, x, gate, g2, sc, sh, wr, *, tm, name):
    m = x.shape[0]

    def rows(a):
        if a.shape[0] == 1:
            return pl.BlockSpec((1, D), lambda i: (0, 0))
        return pl.BlockSpec((tm, D), lambda i: (i, 0))

    blk = pl.BlockSpec((tm, D), lambda i: (i, 0))
    return pl.pallas_call(
        _outproj_kernel,
        grid=(m // tm,),
        in_specs=[blk, pl.BlockSpec((D, D), lambda i: (0, 0)), blk, rows(gate),
                  pl.BlockSpec((1, D), lambda i: (0, 0)), rows(sc), rows(sh),
                  pl.BlockSpec((D, LANES), lambda i: (0, 0))],
        out_specs=[blk, blk, pl.BlockSpec((tm, LANES), lambda i: (i, 0))],
        out_shape=[jax.ShapeDtypeStruct((m, D), F32), jax.ShapeDtypeStruct((m, D), F32),
                   jax.ShapeDtypeStruct((m, LANES), F32)],
        compiler_params=_cparams(("parallel",)),
        name=name,
    )(mix, w, x, gate, g2, sc, sh, wr)


def _route_kernel(lg_ref, bias_ref, tri_ref, eid_ref, gw_ref, rank_ref, cnt_ref, carry):
    i = pl.program_id(0)

    @pl.when(i == 0)
    def _():
        carry[...] = jnp.zeros_like(carry)

    tb = lg_ref.shape[1]
    s = _sigmoid(lg_ref[...])
    sel = s + bias_ref[...]
    epg = N_E // N_G
    r_sel = [sel[e:e + 1, :] for e in range(N_E)]
    r_s = [s[e:e + 1, :] for e in range(N_E)]

    gs = []
    for g in range(N_G):
        a, b, c, d = r_sel[epg * g: epg * g + 4]
        hi01, lo01 = jnp.maximum(a, b), jnp.minimum(a, b)
        hi23, lo23 = jnp.maximum(c, d), jnp.minimum(c, d)
        top1 = jnp.maximum(hi01, hi23)
        top2 = jnp.maximum(jnp.minimum(hi01, hi23), jnp.where(hi01 >= hi23, lo01, lo23))
        gs.append(top1 + top2)
    best = gs[0]
    gidx = jnp.zeros((1, tb), I32)
    for g in range(1, N_G):
        better = gs[g] > best
        gidx = jnp.where(better, g, gidx)
        best = jnp.where(better, gs[g], best)

    def pick_group(rws, j):
        out = rws[(N_G - 1) * epg + j]
        for g in range(N_G - 2, -1, -1):
            out = jnp.where(gidx == g, rws[g * epg + j], out)
        return out

    ing = [pick_group(r_sel, j) for j in range(epg)]
    sg = [pick_group(r_s, j) for j in range(epg)]

    def argmax_first(vals):
        bv, bi = vals[0], jnp.zeros((1, tb), I32)
        for j in range(1, epg):
            better = vals[j] > bv
            bi = jnp.where(better, j, bi)
            bv = jnp.where(better, vals[j], bv)
        return bi

    i1 = argmax_first(ing)
    i2 = argmax_first([jnp.where(i1 == j, -jnp.inf, ing[j]) for j in range(epg)])

    def pick_local(vals, idx):
        out = vals[epg - 1]
        for j in range(epg - 2, -1, -1):
            out = jnp.where(idx == j, vals[j], out)
        return out

    g1, g2 = pick_local(sg, i1), pick_local(sg, i2)
    den = g1 + g2
    e1 = gidx * epg + i1
    e2 = gidx * epg + i2
    eid_ref[0:1, :] = e1
    eid_ref[1:2, :] = e2
    gw_ref[0:1, :] = g1 / den
    gw_ref[1:2, :] = g2 / den

    eio = lax.broadcasted_iota(I32, (N_E, tb), 0)
    oh1 = eio == e1
    oh2 = eio == e2
    member = jnp.where(oh1, 1.0, jnp.where(oh2, 1.0, 0.0))
    before = carry[:, 0:1] + _dot(member.astype(BF16), tri_ref[...])
    rank_ref[0:1, :] = jnp.sum(jnp.where(oh1, before, 0.0), axis=0, keepdims=True).astype(I32)
    rank_ref[1:2, :] = jnp.sum(jnp.where(oh2, before, 0.0), axis=0, keepdims=True).astype(I32)
    carry[...] = carry[...] + jnp.sum(member, axis=1, keepdims=True)
    cnt_ref[...] = carry[...].astype(I32)


def _route_call(logits_t, bias, *, tb):
    t = logits_t.shape[1]
    tri = (jnp.arange(tb)[:, None] < jnp.arange(tb)[None, :]).astype(BF16)
    tok = pl.BlockSpec((2, tb), lambda i: (0, i))
    return pl.pallas_call(
        _route_kernel,
        grid=(t // tb,),
        in_specs=[pl.BlockSpec((N_E, tb), lambda i: (0, i)), pl.BlockSpec((N_E, 1), lambda i: (0, 0)),
                  pl.BlockSpec((tb, tb), lambda i: (0, 0))],
        out_specs=[tok, tok, tok, pl.BlockSpec((N_E, LANES), lambda i: (0, 0))],
        out_shape=[jax.ShapeDtypeStruct((2, t), I32), jax.ShapeDtypeStruct((2, t), F32),
                   jax.ShapeDtypeStruct((2, t), I32), jax.ShapeDtypeStruct((N_E, LANES), I32)],
        scratch_shapes=[pltpu.VMEM((N_E, LANES), F32)],
        compiler_params=_cparams(("arbitrary",)),
        name="route",
    )(logits_t, bias.reshape(N_E, 1), tri)


def _row_copy(src_ref, src_row, dst_ref, dst_row, sem):
    return pltpu.make_async_copy(src_ref.at[pl.ds(src_row, 1)], dst_ref.at[pl.ds(dst_row, 1)], sem)


def _dispatch_kernel(tok_ref, h_ref, o_ref, sem, *, tg):
    base = pl.program_id(0) * tg

    def issue(r, c):
        _row_copy(h_ref, tok_ref[base + r], o_ref, r, sem).start()
        return c

    lax.fori_loop(0, tg, issue, 0)

    def drain(r, c):
        _row_copy(h_ref, 0, o_ref, r, sem).wait()
        return c

    lax.fori_loop(0, tg, drain, 0)


def _dispatch_call(slot_token, h, *, tg):
    n_slots = slot_token.shape[0]
    return pl.pallas_call(
        functools.partial(_dispatch_kernel, tg=tg),
        grid_spec=pltpu.PrefetchScalarGridSpec(
            num_scalar_prefetch=1,
            grid=(n_slots // tg,),
            in_specs=[pl.BlockSpec(memory_space=pl.ANY)],
            out_specs=pl.BlockSpec((tg, D), lambda i, tok: (i, 0)),
            scratch_shapes=[pltpu.SemaphoreType.DMA(())],
        ),
        out_shape=jax.ShapeDtypeStruct((n_slots, D), F32),
        compiler_params=_cparams(("arbitrary",)),
        name="moe_dispatch",
    )(slot_token, h)


def _gateup_kernel(ie_ref, in_ref, im_ref, iv_ref, x_ref, wg_ref, wu_ref, o_ref):
    @pl.when(iv_ref[pl.program_id(0)] == 1)
    def _():
        xb = x_ref[...].astype(BF16)
        g = _dot(xb, wg_ref[0].astype(BF16))
        u = _dot(xb, wu_ref[0].astype(BF16))
        o_ref[...] = ((g * _sigmoid(g)) * u).astype(BF16)

    @pl.when(iv_ref[pl.program_id(0)] == 0)
    def _():
        o_ref[...] = jnp.zeros_like(o_ref)


def _down_kernel(ie_ref, in_ref, im_ref, iv_ref, h_ref, wd_ref, o_ref):
    @pl.when(iv_ref[pl.program_id(0)] == 1)
    def _():
        o_ref[...] = _dot(h_ref[...], wd_ref[0].astype(BF16))

    @pl.when(iv_ref[pl.program_id(0)] == 0)
    def _():
        o_ref[...] = jnp.zeros_like(o_ref)


def _gateup_call(items, x_sorted, wg, wu, *, tm, tn):
    n_slots = x_sorted.shape[0]
    n_items = items[0].shape[0]
    w_spec = pl.BlockSpec((1, D, tn), lambda i, ie, inn, im, iv: (ie[i], 0, inn[i]))
    return pl.pallas_call(
        _gateup_kernel,
        grid_spec=pltpu.PrefetchScalarGridSpec(
            num_scalar_prefetch=4,
            grid=(n_items,),
            in_specs=[pl.BlockSpec((tm, D), lambda i, ie, inn, im, iv: (im[i], 0)), w_spec, w_spec],
            out_specs=pl.BlockSpec((tm, tn), lambda i, ie, inn, im, iv: (im[i], inn[i])),
        ),
        out_shape=jax.ShapeDtypeStruct((n_slots, FF), BF16),
        compiler_params=_cparams(("arbitrary",)),
        name="moe_gateup",
    )(*items, x_sorted, wg, wu)


def _down_call(items, hs, wd, *, tm, tn):
    n_slots = hs.shape[0]
    n_items = items[0].shape[0]
    return pl.pallas_call(
        _down_kernel,
        grid_spec=pltpu.PrefetchScalarGridSpec(
            num_scalar_prefetch=4,
            grid=(n_items,),
            in_specs=[pl.BlockSpec((tm, FF), lambda i, ie, inn, im, iv: (im[i], 0)),
                      pl.BlockSpec((1, FF, tn), lambda i, ie, inn, im, iv: (ie[i], 0, inn[i]))],
            out_specs=pl.BlockSpec((tm, tn), lambda i, ie, inn, im, iv: (im[i], inn[i])),
        ),
        out_shape=jax.ShapeDtypeStruct((n_slots, D), F32),
        compiler_params=_cparams(("arbitrary",)),
        name="moe_down",
    )(*items, hs, wd)


def _moe_items(blocks_per_expert, n_tiles, n_blocks):
    bends = jnp.cumsum(blocks_per_expert)
    bstart = bends - blocks_per_expert
    total = bends[-1] * n_tiles
    step = jnp.arange(n_blocks * n_tiles, dtype=I32)
    j = jnp.minimum(step, total - 1)
    e = jnp.clip(jnp.searchsorted(bends * n_tiles, j, side="right"), 0, N_E - 1).astype(I32)
    local = j - bstart[e] * n_tiles
    nb = jnp.maximum(blocks_per_expert[e], 1)
    valid = step < total
    tail = jnp.maximum(step - total, 0)
    col = jnp.where(valid, local // nb, tail % n_tiles)
    row = jnp.where(valid, bstart[e] + local % nb, bends[-1] + tail // n_tiles)
    return e, col.astype(I32), row.astype(I32), valid.astype(I32)


def _combine_kernel(dest_ref, y_ref, x_ref, gm_ref, gw_ref, o_ref, buf, sem, *, tb, t_off, t_total):
    base = t_off + pl.program_id(0) * tb

    def issue(r, c):
        for k in range(2):
            _row_copy(y_ref, dest_ref[k * t_total + base + r], buf.at[k], r, sem).start()
        return c

    lax.fori_loop(0, tb, issue, 0)

    def drain(r, c):
        for k in range(2):
            _row_copy(y_ref, 0, buf.at[k], r, sem).wait()
        return c

    lax.fori_loop(0, tb, drain, 0)
    gw = gw_ref[...]
    o_ref[...] = x_ref[...] + gm_ref[...] * (gw[:, 0:1] * buf[0] + gw[:, 1:2] * buf[1])


def _combine_call(dest_flat, y, x, gm, gw, *, tb, t_off, t_total, name):
    m = x.shape[0]
    if gm.shape[0] == 1:
        gm_spec = pl.BlockSpec((1, D), lambda i, d: (0, 0))
    else:
        gm_spec = pl.BlockSpec((tb, D), lambda i, d: (i, 0))
    return pl.pallas_call(
        functools.partial(_combine_kernel, tb=tb, t_off=t_off, t_total=t_total),
        grid_spec=pltpu.PrefetchScalarGridSpec(
            num_scalar_prefetch=1,
            grid=(m // tb,),
            in_specs=[pl.BlockSpec(memory_space=pl.ANY),
                      pl.BlockSpec((tb, D), lambda i, d: (i, 0)), gm_spec,
                      pl.BlockSpec((tb, 2), lambda i, d: (i, 0))],
            out_specs=pl.BlockSpec((tb, D), lambda i, d: (i, 0)),
            scratch_shapes=[pltpu.VMEM((2, tb, D), F32), pltpu.SemaphoreType.DMA(())],
        ),
        out_shape=jax.ShapeDtypeStruct((m, D), F32),
        compiler_params=_cparams(("arbitrary",)),
        name=name,
    )(dest_flat, y, x, gm, gw)


def _moe(h_all, logits_all, w_router_bias, wg, wu, wd, xp, xs, gmp, gms):
    t = h_all.shape[0]
    tm = MOE_TM
    eid, gw, rank, cnt = _route_call(logits_all[:, :N_E].T, w_router_bias, tb=640)
    counts = cnt[:, 0]
    blocks_per_expert = (counts + tm - 1) // tm
    pstart = (jnp.cumsum(blocks_per_expert) - blocks_per_expert) * tm
    dest = pstart[eid] + rank
    n_blocks = (2 * t) // tm + N_E
    n_slots = n_blocks * tm
    tok = jnp.tile(jnp.arange(t, dtype=I32), 2)
    slot_token = jnp.zeros((n_slots,), I32).at[dest.reshape(-1)].set(tok)
    x_sorted = _dispatch_call(slot_token, h_all, tg=tm)
    items1 = _moe_items(blocks_per_expert, FF // MOE_FF_TN, n_blocks)
    hs = _gateup_call(items1, x_sorted, wg, wu, tm=tm, tn=MOE_FF_TN)
    items2 = _moe_items(blocks_per_expert, D // MOE_D_TN, n_blocks)
    y = _down_call(items2, hs, wd, tm=tm, tn=MOE_D_TN)
    dest_flat = dest.reshape(-1)
    gwt = gw.T
    np_ = xp.shape[0]
    xp2 = _combine_call(dest_flat, y, xp, gmp, gwt[:np_], tb=128, t_off=0, t_total=t, name="moe_combine_p")
    xs2 = _combine_call(dest_flat, y, xs, gms, gwt[np_:], tb=128, t_off=np_, t_total=t, name="moe_combine_s")
    return xp2, xs2


def _attn_kernel(lam_ref, q_ref, k_ref, v_ref, g_ref, o_ref, m_scr, l_scr, acc_scr, *, tq, out_scale):
    qi = pl.program_id(1)
    q = q_ref[...]
    lane = lax.broadcasted_iota(I32, (tq, C_VD), 1)
    zero = jnp.zeros_like(q)
    qq = jnp.concatenate([jnp.where(lane < C_HD, q, zero), jnp.where(lane >= C_HD, q, zero)], axis=0)
    m_scr[...] = jnp.full_like(m_scr, -jnp.inf)
    l_scr[...] = jnp.zeros_like(l_scr)
    acc_scr[...] = jnp.zeros_like(acc_scr)

    def step(j, masked):
        off = pl.multiple_of(j * tq, tq)
        kb = k_ref[pl.ds(off, tq), :]
        vb = v_ref[pl.ds(off, tq), :]
        s = _dot_nt(qq, kb)
        if masked:
            r = lax.broadcasted_iota(I32, (2 * tq, tq), 0)
            c = lax.broadcasted_iota(I32, (2 * tq, tq), 1)
            s = jnp.where(c <= jnp.where(r >= tq, r - tq, r), s, -jnp.inf)
        m_prev = m_scr[...]
        m_new = jnp.maximum(m_prev, jnp.max(s, axis=1, keepdims=True))
        alpha = jnp.exp(m_prev - m_new)
        p = jnp.exp(s - m_new)
        l_scr[...] = alpha * l_scr[...] + jnp.sum(p, axis=1, keepdims=True)
        acc_scr[...] = alpha * acc_scr[...] + _dot(p.astype(BF16), vb)
        m_scr[...] = m_new

    def body(j, c):
        step(j, False)
        return c

    lax.fori_loop(0, qi, body, 0)
    step(qi, True)
    o = acc_scr[0:tq, :] / l_scr[0:tq, :] - lam_ref[0] * (acc_scr[tq:2 * tq, :] / l_scr[tq:2 * tq, :])
    on = o * lax.rsqrt(jnp.mean(o * o, axis=-1, keepdims=True) + EPS) * g_ref[...] * out_scale
    o_ref[...] = on.astype(o_ref.dtype)


def _attn_call(lam, q, k, v, g, *, tq, out_scale):
    s = q.shape[0]
    return pl.pallas_call(
        functools.partial(_attn_kernel, tq=tq, out_scale=out_scale),
        grid=(C_H, s // tq),
        in_specs=[
            pl.BlockSpec(memory_space=pltpu.SMEM),
            pl.BlockSpec((tq, C_VD), lambda h, i: (i, h)),
            pl.BlockSpec((s, C_VD), lambda h, i: (0, h)),
            pl.BlockSpec((s, C_VD), lambda h, i: (0, h)),
            pl.BlockSpec((1, C_VD), lambda h, i: (0, 0)),
        ],
        out_specs=pl.BlockSpec((tq, C_VD), lambda h, i: (i, h)),
        out_shape=jax.ShapeDtypeStruct((s, C_H * C_VD), BF16),
        scratch_shapes=[pltpu.VMEM((2 * tq, 1), F32), pltpu.VMEM((2 * tq, 1), F32),
                        pltpu.VMEM((2 * tq, C_VD), F32)],
        compiler_params=_cparams(("parallel", "arbitrary")),
        name="diff_attn_prompt",
    )(lam, q, k, v, g)


def _sattn_kernel(pt_ref, lam_ref, wq_ref, kp_ref, vp_ref, kn_ref, vn_ref, g_ref, o_ref,
                  m_scr, l_scr, acc_scr, *, out_scale):
    b = pl.program_id(0)
    p = pl.program_id(1)

    @pl.when(p == 0)
    def _():
        m_scr[...] = jnp.full_like(m_scr, -jnp.inf)
        l_scr[...] = jnp.zeros_like(l_scr)
        acc_scr[...] = jnp.zeros_like(acc_scr)

    wq = wq_ref[0]

    def update(s, vmat):
        m_prev = m_scr[...]
        m_new = jnp.maximum(m_prev, jnp.max(s, axis=1, keepdims=True))
        alpha = jnp.exp(m_prev - m_new)
        pr = jnp.exp(s - m_new)
        l_scr[...] = alpha * l_scr[...] + jnp.sum(pr, axis=1, keepdims=True)
        acc_scr[...] = alpha * acc_scr[...] + _dot(pr.astype(BF16), vmat)
        m_scr[...] = m_new

    update(_dot_nt(wq, kp_ref[0].astype(BF16)), vp_ref[0].astype(BF16))

    @pl.when(p == pl.num_programs(1) - 1)
    def _():
        n = kn_ref.shape[0]
        s = _dot_nt(wq, kn_ref[...])
        c = lax.broadcasted_iota(I32, (n, n), 0)
        j = lax.broadcasted_iota(I32, (n, n), 1)
        same_batch = lax.shift_right_logical(j, 2) == b
        ok = same_batch & ((j & (DEC_S - 1)) <= (c & (DEC_S - 1)))
        update(jnp.where(ok, s, -jnp.inf), vn_ref[...])
        rows = 2 * DEC_S
        for h in range(C_H):
            rs = slice(h * rows, (h + 1) * rows)
            cs = slice(h * C_VD, (h + 1) * C_VD)
            blk = acc_scr[rs, cs] / l_scr[rs, :]
            o = blk[0:DEC_S, :] - lam_ref[0] * blk[DEC_S:rows, :]
            on = o * lax.rsqrt(jnp.mean(o * o, axis=-1, keepdims=True) + EPS) * g_ref[...] * out_scale
            o_ref[0, :, cs] = on


def _sattn_call(page_table, lam, wq, ck, cv, kn, vn, g, *, out_scale):
    nb, n_pages = page_table.shape
    ncol = wq.shape[1]
    wide = ck.shape[2]
    return pl.pallas_call(
        functools.partial(_sattn_kernel, out_scale=out_scale),
        grid_spec=pltpu.PrefetchScalarGridSpec(
            num_scalar_prefetch=1,
            grid=(nb, n_pages),
            in_specs=[
                pl.BlockSpec(memory_space=pltpu.SMEM),
                pl.BlockSpec((1, ncol, wide), lambda b, p, pt: (b, 0, 0)),
                pl.BlockSpec((1, PAGE, wide), lambda b, p, pt: (pt[b * n_pages + p], 0, 0)),
                pl.BlockSpec((1, PAGE, wide), lambda b, p, pt: (pt[b * n_pages + p], 0, 0)),
                pl.BlockSpec(kn.shape, lambda b, p, pt: (0, 0)),
                pl.BlockSpec(vn.shape, lambda b, p, pt: (0, 0)),
                pl.BlockSpec((1, C_VD), lambda b, p, pt: (0, 0)),
            ],
            out_specs=pl.BlockSpec((1, DEC_S, wide), lambda b, p, pt: (b, 0, 0)),
            scratch_shapes=[pltpu.VMEM((ncol, 1), F32), pltpu.VMEM((ncol, 1), F32),
                            pltpu.VMEM((ncol, wide), F32)],
        ),
        out_shape=jax.ShapeDtypeStruct((nb, DEC_S, wide), F32),
        compiler_params=_cparams(("parallel", "arbitrary")),
        name="diff_attn_sample",
    )(page_table.reshape(-1), lam, wq, ck, cv, kn, vn, g)


def _rope0_tables(pos):
    half = B_DK // 2
    freqs = 10000.0 ** (-jnp.arange(half, dtype=F32) / half)
    ang = pos.astype(F32)[:, None] * freqs[None, :]
    return jnp.cos(ang), jnp.sin(ang)


def _rope1_tables(pos):
    half = C_ROT // 2
    freqs = 500000.0 ** (-jnp.arange(half, dtype=F32) / half)
    ang = pos.astype(F32)[:, None] * freqs[None, :]
    cos, sin = jnp.cos(ang), jnp.sin(ang)
    n = pos.shape[0]
    pad = C_HD - C_ROT
    cos_t = jnp.concatenate([cos, cos, jnp.ones((n, pad), F32)], axis=1)
    sin_a = jnp.concatenate([-sin, jnp.zeros((n, half + pad), F32)], axis=1)
    sin_b = jnp.concatenate([jnp.zeros((n, half), F32), sin, jnp.zeros((n, pad), F32)], axis=1)
    rep = lambda t: jnp.tile(t, (1, LANES // C_HD))
    return rep(cos_t), rep(sin_a), rep(sin_b)


def _log_gamma():
    return jnp.log(1.0 - 2.0 ** (-5.0 - jnp.arange(B_H, dtype=F32)))


def _lanes_per_head(per_head):
    return jnp.repeat(per_head, B_DK, axis=1)


def kernel(x_prompt, x_sample, c_prompt, c_sample, state_ret, cache_k, cache_v, page_table, w_ada, b_ada, norm_mix, norm_ffn, w_in0, gmlp_ln_g, gmlp_ln_b, gmlp_w_s, gmlp_b_s, ret_norm_g, w_out0, w_in1, q_norm_g, k_norm_g, lambda_q1, lambda_k1, lambda_q2, lambda_k2, subln_g, w_out1, w_router, router_bias, w_gate, w_up, w_down):
    xp = x_prompt.reshape(SEQ, D)
    xs = x_sample.reshape(NS, D)
    pos_p = jnp.arange(SEQ)
    pos_s = PAST + jnp.arange(DEC_S)

    mc = 1 + DEC_B
    mc_pad = -(-mc // SUBLANES) * SUBLANES
    c_all = jnp.concatenate([c_prompt, c_sample, jnp.zeros((mc_pad - mc, D), F32)], axis=0)
    mods = _ada_call(c_all, w_ada, b_ada).reshape(2, mc_pad, 6, D)
    wr_pad = jnp.pad(w_router, ((0, 0), (0, LANES - N_E)))

    def mod(l, i):
        return mods[l, 0:1, i], jnp.repeat(mods[l, 1:mc, i], DEC_S, axis=0)

    l = 0
    (shp, shs), (scp, scs), (gtp, gts) = mod(l, 0), mod(l, 1), mod(l, 2)
    (sh2p, sh2s), (sc2p, sc2s), (gt2p, gt2s) = mod(l, 3), mod(l, 4), mod(l, 5)
    gmix = norm_mix[l].reshape(1, D)
    w_in = w_in0[0].astype(BF16)
    pp = _inproj_call(xp, gmix, scp, shp, w_in, mode="f32", tm=512, tn=1024, name="inproj0_p")[0]
    ps = _inproj_call(xs, gmix, scs, shs, w_in, mode="f32", tm=NS, tn=1024, name="inproj0_s")[0]

    log_g = _log_gamma()
    lng, lnb = gmlp_ln_g[0].reshape(1, A_W), gmlp_ln_b[0].reshape(1, A_W)
    rg = ret_norm_g[0].reshape(1, B_H * B_DK)
    ch = 128
    idx = jnp.arange(ch, dtype=F32)
    rel = idx[:, None] - idx[None, :]
    dec = jnp.where(rel[None] >= 0, jnp.exp(jnp.maximum(rel, 0.0)[None] * log_g[:, None, None]), 0.0)
    cd = _lanes_per_head(jnp.exp((idx + 1.0)[:, None] * log_g[None, :]))
    kd = _lanes_per_head(jnp.exp((ch - 1.0 - idx)[:, None] * log_g[None, :]))
    sdec = jnp.exp(ch * log_g)
    wm = jnp.where(jnp.tril(jnp.ones((ch, ch), dtype=bool))[None], gmlp_w_s[0], 0.0).astype(BF16)
    bs = jnp.repeat(gmlp_b_s[0].T, A_W // A_H, axis=1)
    cos0p, sin0p = _rope0_tables(pos_p)
    mix_p, ret_p = _mix0p_call(pp, lng, lnb, wm, bs, cos0p, sin0p, dec, cd, kd, rg, sdec)

    rows = 2 * DEC_S
    tloc = jnp.arange(rows) % DEC_S
    ws4 = gmlp_w_s[0][:, :DEC_S, :DEC_S]
    cw = []
    dm = []
    for d in range(DEC_S):
        src = tloc - d
        ok = src >= 0
        coef = jnp.where(ok[:, None], ws4[:, tloc, jnp.maximum(src, 0)].T, 0.0)
        cw.append(jnp.repeat(coef, A_W // A_H, axis=1))
        dm.append(_lanes_per_head(jnp.where(ok[:, None], jnp.exp(float(d) * log_g)[None, :], 0.0)))
    cw, dm = jnp.stack(cw), jnp.stack(dm)
    bs_s = jnp.repeat(gmlp_b_s[0][:, :DEC_S].T[tloc], A_W // A_H, axis=1)
    tl = tloc.astype(F32)
    cd_s = _lanes_per_head(jnp.exp((tl + 1.0)[:, None] * log_g[None, :]))
    kd_s = _lanes_per_head(jnp.exp((DEC_S - 1.0 - tl)[:, None] * log_g[None, :]))
    sdec_s = jnp.exp(DEC_S * log_g)
    cos0s, sin0s = _rope0_tables(pos_s)
    mix_s, gv_s, ret_s = _mix0s_call(ps, state_ret[0], lng, lnb, cw, bs_s, cos0s[tloc], sin0s[tloc],
                                     dm, cd_s, kd_s, rg, sdec_s)

    gffn = norm_ffn[l].reshape(1, D)
    w_out = w_out0[0].astype(BF16)
    xp, h2p, lgp = _outproj_call(mix_p, w_out, xp, gtp, gffn, sc2p, sh2p, wr_pad, tm=256, name="outproj0_p")
    xs, h2s, lgs = _outproj_call(mix_s, w_out, xs, gts, gffn, sc2s, sh2s, wr_pad, tm=NS, name="outproj0_s")
    xp, xs = _moe(jnp.concatenate([h2p, h2s], axis=0), jnp.concatenate([lgp, lgs], axis=0), router_bias,
                  w_gate[l], w_up[l], w_down[l], xp, xs, gt2p, gt2s)

    l = 1
    (shp, shs), (scp, scs), (gtp, gts) = mod(l, 0), mod(l, 1), mod(l, 2)
    (sh2p, sh2s), (sc2p, sc2s), (gt2p, gt2s) = mod(l, 3), mod(l, 4), mod(l, 5)
    gmix = norm_mix[l].reshape(1, D)
    lambda_init = 0.8 - 0.6 * math.exp(-0.3 * l)
    lam = (jnp.exp(jnp.sum(lambda_q1[0] * lambda_k1[0])) - jnp.exp(jnp.sum(lambda_q2[0] * lambda_k2[0]))
           + lambda_init).reshape(1).astype(F32)
    hw = C_H * C_VD
    wq_w = w_in1[0][:, 0:hw].astype(BF16)
    wk_w = w_in1[0][:, hw:2 * hw].astype(BF16)
    wv_w = w_in1[0][:, 2 * hw:3 * hw].astype(BF16)
    gq = jnp.tile(q_norm_g[0], LANES // C_HD).reshape(1, LANES)
    gk = jnp.tile(k_norm_g[0], LANES // C_HD).reshape(1, LANES)
    lane = jnp.arange(LANES)
    bd = (lane[:, None] // C_HD == lane[None, :] // C_HD).astype(BF16)
    tabs_p = _rope1_tables(pos_p)
    tabs_s = tuple(jnp.tile(t, (DEC_B, 1)) for t in _rope1_tables(pos_s))
    qscale = C_HD ** -0.5

    q_p = _inproj_call(xp, gmix, scp, shp, wq_w, mode="q", tm=512, tn=1024, extras=(gq, *tabs_p, bd),
                       scale=qscale, name="inproj1_q_p")[0]
    k_p, k_pb = _inproj_call(xp, gmix, scp, shp, wk_w, mode="k", tm=512, tn=1024, extras=(gk, *tabs_p, bd),
                             name="inproj1_k_p")
    v_p, v_pb = _inproj_call(xp, gmix, scp, shp, wv_w, mode="dual", tm=512, tn=1024, name="inproj1_v_p")
    q_s = _inproj_call(xs, gmix, scs, shs, wq_w, mode="q", tm=NS, tn=1024, extras=(gq, *tabs_s, bd),
                       scale=qscale, name="inproj1_q_s")[0]
    k_s, k_sb = _inproj_call(xs, gmix, scs, shs, wk_w, mode="k", tm=NS, tn=1024, extras=(gk, *tabs_s, bd),
                             name="inproj1_k_s")
    v_s, v_sb = _inproj_call(xs, gmix, scs, shs, wv_w, mode="dual", tm=NS, tn=1024, name="inproj1_v_s")

    gsub = subln_g[0].reshape(1, C_VD)
    out_scale = 1.0 - lambda_init
    att_p = _attn_call(lam, q_p, k_pb, v_pb, gsub, tq=ATT_TQ, out_scale=out_scale)

    q6 = q_s.reshape(DEC_B, DEC_S, C_H, 2, C_HD).transpose(0, 2, 3, 1, 4)
    eye_h = jnp.eye(C_H, dtype=BF16)
    eye_r = jnp.eye(2, dtype=BF16)
    wq_blk = (q6[:, :, :, :, None, None, :] * eye_h[None, :, None, None, :, None, None]
              * eye_r[None, None, :, None, None, :, None]).reshape(DEC_B, C_H * 2 * DEC_S, hw)
    n_pool = cache_k.shape[1]
    att_s = _sattn_call(page_table, lam, wq_blk, cache_k[0].reshape(n_pool, PAGE, hw),
                        cache_v[0].reshape(n_pool, PAGE, hw), k_sb, v_sb, gsub, out_scale=out_scale)
    att_s = att_s.reshape(NS, hw)

    gffn = norm_ffn[l].reshape(1, D)
    w_out = w_out1[0].astype(BF16)
    xp, h2p, lgp = _outproj_call(att_p, w_out, xp, gtp, gffn, sc2p, sh2p, wr_pad, tm=256, name="outproj1_p")
    xs, h2s, lgs = _outproj_call(att_s, w_out, xs, gts, gffn, sc2s, sh2s, wr_pad, tm=NS, name="outproj1_s")
    xp, xs = _moe(jnp.concatenate([h2p, h2s], axis=0), jnp.concatenate([lgp, lgs], axis=0), router_bias,
                  w_gate[l], w_up[l], w_down[l], xp, xs, gt2p, gt2s)

    return (xp.reshape(1, SEQ, D), xs.reshape(DEC_B, DEC_S, D),
            ret_p.reshape(1, 1, B_H, B_DK, B_DK), ret_s.reshape(1, DEC_B, B_H, B_DK, B_DK),
            gv_s.reshape(1, DEC_B, DEC_S, A_W),
            k_p.reshape(1, 1, SEQ, C_H, C_VD), v_p.reshape(1, 1, SEQ, C_H, C_VD),
            k_s.reshape(1, DEC_B, DEC_S, C_H, C_VD), v_s.reshape(1, DEC_B, DEC_S, C_H, C_VD))
```

```python
import functools
import math

import jax
import jax.numpy as jnp
from jax import lax
from jax.experimental import pallas as pl
from jax.experimental.pallas import tpu as pltpu

F32 = jnp.float32
BF16 = jnp.bfloat16
I32 = jnp.int32
HIGHEST = lax.Precision.HIGHEST

D = 2048
SEQ = 8192
DEC_B = 32
DEC_S = 4
NS = DEC_B * DEC_S
PAST = 8192
PAGE = 128
EPS = 1e-6
A_W = 1024
A_H = 8
B_H = 4
B_DK = 256
C_H = 16
C_HD = 64
C_VD = 128
C_ROT = 16
N_E = 16
N_G = 4
FF = 1536

LANES = 128
SUBLANES = 8
VMEM_LIMIT_BYTES = 56 * 1024 * 1024

MOE_TM = 256
MOE_FF_TN = 768
MOE_D_TN = 1024
MOE_DISPATCH_ROWS = 3 * MOE_TM
ATT_TQ = 512
SATT_PAGES = 4


def _cparams(sem):
    return pltpu.CompilerParams(dimension_semantics=sem, vmem_limit_bytes=VMEM_LIMIT_BYTES)


def _sigmoid(x):
    return 1.0 / (1.0 + jnp.exp(-x))


def _gelu(x):
    return x * (0.5 * (1.0 + jnp.tanh(0.7978845608028654 * (x + 0.044715 * (x * x * x)))))


def _rms_mod(x, g, sc, sh):
    ms = jnp.mean(x * x, axis=-1, keepdims=True)
    return (x * lax.rsqrt(ms + EPS)) * g * (1.0 + sc) + sh


def _dot(a, b):
    return jnp.dot(a, b, preferred_element_type=F32)


def _dot_nt(a, b):
    return lax.dot_general(a, b, (((1,), (1,)), ((), ())), preferred_element_type=F32)


def _dot_tn(a, b):
    return lax.dot_general(a, b, (((0,), (0,)), ((), ())), preferred_element_type=F32)


def _ada_kernel(c_ref, w_ref, b_ref, o_ref):
    c = c_ref[...]
    s = c * _sigmoid(c)
    o_ref[0] = jnp.dot(s, w_ref[0], preferred_element_type=F32, precision=HIGHEST) + b_ref[0]


def _ada_call(c_all, w_ada, b_ada):
    nl, _, n = w_ada.shape
    mc = c_all.shape[0]
    tn = 1024
    return pl.pallas_call(
        _ada_kernel,
        grid=(nl, n // tn),
        in_specs=[
            pl.BlockSpec((mc, D), lambda l, j: (0, 0)),
            pl.BlockSpec((1, D, tn), lambda l, j: (l, 0, j)),
            pl.BlockSpec((1, 1, tn), lambda l, j: (l, 0, j)),
        ],
        out_specs=pl.BlockSpec((1, mc, tn), lambda l, j: (l, 0, j)),
        out_shape=jax.ShapeDtypeStruct((nl, mc, n), F32),
        compiler_params=_cparams(("parallel", "parallel")),
        name="ada",
    )(c_all, w_ada, b_ada.reshape(nl, 1, n))


def _inproj_kernel(*refs, mode, scale, tn):
    x_ref, g_ref, sc_ref, sh_ref, w_ref = refs[:5]
    h_scr = refs[-1]

    @pl.when(pl.program_id(1) == 0)
    def _():
        h_scr[...] = _rms_mod(x_ref[...], g_ref[...], sc_ref[...], sh_ref[...]).astype(BF16)

    acc = _dot(h_scr[...], w_ref[...])
    if mode in ("q", "k"):
        gq_ref, cos_ref, sa_ref, sb_ref, bd_ref = refs[5:10]
        outs = refs[10:-1]
        gq = gq_ref[...]
        cos_t, sin_a, sin_b, bd = cos_ref[...], sa_ref[...], sb_ref[...], bd_ref[...]
        for j in range(tn // LANES):
            sl = slice(j * LANES, (j + 1) * LANES)
            y = acc[:, sl]
            sq = y * y
            hi = sq.astype(BF16)
            lo = (sq - hi.astype(F32)).astype(BF16)
            ss = _dot(hi, bd) + _dot(lo, bd)
            yn = y * lax.rsqrt(ss * (1.0 / C_HD) + EPS) * gq
            r = yn * cos_t + pltpu.roll(yn, LANES - C_ROT // 2, 1) * sin_a + pltpu.roll(yn, C_ROT // 2, 1) * sin_b
            if scale != 1.0:
                r = r * scale
            if mode == "q":
                outs[0][:, sl] = r.astype(BF16)
            else:
                outs[0][:, sl] = r
                outs[1][:, sl] = r.astype(BF16)
    elif mode == "dual":
        outs = refs[5:-1]
        outs[0][...] = acc
        outs[1][...] = acc.astype(BF16)
    else:
        refs[5][...] = acc


def _inproj_call(x, g, sc, sh, w, *, mode, tm, tn, extras=(), scale=1.0, name="inproj"):
    m = x.shape[0]
    n = w.shape[1]

    def rows(a):
        if a.shape[0] == 1:
            return pl.BlockSpec((1, D), lambda i, j: (0, 0))
        return pl.BlockSpec((tm, D), lambda i, j: (i, 0))

    in_specs = [
        pl.BlockSpec((tm, D), lambda i, j: (i, 0)),
        pl.BlockSpec((1, D), lambda i, j: (0, 0)),
        rows(sc),
        rows(sh),
        pl.BlockSpec((D, tn), lambda i, j: (0, j)),
    ]
    args = [x, g, sc, sh, w]
    o_spec = pl.BlockSpec((tm, tn), lambda i, j: (i, j))
    if mode in ("q", "k"):
        gq, cos_t, sin_a, sin_b, bd = extras
        in_specs += [
            pl.BlockSpec((1, LANES), lambda i, j: (0, 0)),
            pl.BlockSpec((tm, LANES), lambda i, j: (i, 0)),
            pl.BlockSpec((tm, LANES), lambda i, j: (i, 0)),
            pl.BlockSpec((tm, LANES), lambda i, j: (i, 0)),
            pl.BlockSpec((LANES, LANES), lambda i, j: (0, 0)),
        ]
        args += [gq, cos_t, sin_a, sin_b, bd]
    if mode == "q":
        out_shape = [jax.ShapeDtypeStruct((m, n), BF16)]
    elif mode in ("k", "dual"):
        out_shape = [jax.ShapeDtypeStruct((m, n), F32), jax.ShapeDtypeStruct((m, n), BF16)]
    else:
        out_shape = [jax.ShapeDtypeStruct((m, n), F32)]
    res = pl.pallas_call(
        functools.partial(_inproj_kernel, mode=mode, scale=scale, tn=tn),
        grid=(m // tm, n // tn),
        in_specs=in_specs,
        out_specs=[o_spec] * len(out_shape),
        out_shape=out_shape,
        scratch_shapes=[pltpu.VMEM((tm, D), BF16)],
        compiler_params=_cparams(("parallel", "arbitrary")),
        name=name,
    )(*args)
    return res


def _mix0p_kernel(sdec_ref, p_ref, lng_ref, lnb_ref, wm_ref, bs_ref, cos_ref, sin_ref, dec_ref,
                  cd_ref, kd_ref, rg_ref, mix_ref, so_ref, st_scr):
    c = pl.program_id(0)

    @pl.when(c == 0)
    def _():
        st_scr[...] = jnp.zeros_like(st_scr)

    u = _gelu(p_ref[:, 0:A_W])
    v = _gelu(p_ref[:, A_W:2 * A_W])
    mu = jnp.mean(v, axis=-1, keepdims=True)
    vc = v - mu
    v = vc * lax.rsqrt(jnp.mean(vc * vc, axis=-1, keepdims=True) + EPS) * lng_ref[...] + lnb_ref[...]
    vb = v.astype(BF16)
    for hh in range(A_H):
        sl = slice(hh * LANES, (hh + 1) * LANES)
        sv = _dot(wm_ref[hh], vb[:, sl]) + bs_ref[:, sl]
        mix_ref[:, sl] = (u[:, sl] * sv).astype(BF16)

    cos, sin = cos_ref[...], sin_ref[...]
    half = B_DK // 2

    def rope(t):
        t1, t2 = t[:, :half], t[:, half:]
        return jnp.concatenate([t1 * cos - t2 * sin, t2 * cos + t1 * sin], axis=1)

    base = 2 * A_W
    hw = B_H * B_DK
    for hd in range(B_H):
        sl = slice(hd * B_DK, (hd + 1) * B_DK)
        q = rope(p_ref[:, base + hd * B_DK: base + (hd + 1) * B_DK])
        k = rope(p_ref[:, base + hw + hd * B_DK: base + hw + (hd + 1) * B_DK]) * (B_DK ** -0.5)
        vr = p_ref[:, base + 2 * hw + hd * B_DK: base + 2 * hw + (hd + 1) * B_DK]
        g = p_ref[:, base + 3 * hw + hd * B_DK: base + 3 * hw + (hd + 1) * B_DK]
        qb, kb, vrb = q.astype(BF16), k.astype(BF16), vr.astype(BF16)
        inner = _dot_nt(qb, kb) * dec_ref[hd]
        st = st_scr[hd]
        o = _dot(inner.astype(BF16), vrb) + _dot(qb, st.astype(BF16)) * cd_ref[:, sl]
        kdb = (k * kd_ref[:, sl]).astype(BF16)
        st_scr[hd] = sdec_ref[hd] * st + _dot_tn(kdb, vrb)
        on = o * lax.rsqrt(jnp.mean(o * o, axis=-1, keepdims=True) + EPS) * rg_ref[:, sl]
        mix_ref[:, A_W + hd * B_DK: A_W + (hd + 1) * B_DK] = ((g * _sigmoid(g)) * on).astype(BF16)

    @pl.when(c == pl.num_programs(0) - 1)
    def _():
        so_ref[...] = st_scr[...]


def _mix0p_call(p, lng, lnb, wm, bs, cos, sin, dec, cd, kd, rg, sdec):
    m = p.shape[0]
    ch = 128
    full = lambda shape: pl.BlockSpec(shape, lambda c: (0,) * len(shape))
    return pl.pallas_call(
        _mix0p_kernel,
        grid=(m // ch,),
        in_specs=[
            pl.BlockSpec(memory_space=pltpu.SMEM),
            pl.BlockSpec((ch, p.shape[1]), lambda c: (c, 0)),
            full((1, A_W)), full((1, A_W)),
            full((A_H, ch, ch)), full((ch, A_W)),
            pl.BlockSpec((ch, LANES), lambda c: (c, 0)),
            pl.BlockSpec((ch, LANES), lambda c: (c, 0)),
            full((B_H, ch, ch)), full((ch, B_H * B_DK)), full((ch, B_H * B_DK)),
            full((1, B_H * B_DK)),
        ],
        out_specs=[
            pl.BlockSpec((ch, D), lambda c: (c, 0)),
            full((B_H, B_DK, B_DK)),
        ],
        out_shape=[jax.ShapeDtypeStruct((m, D), BF16), jax.ShapeDtypeStruct((B_H, B_DK, B_DK), F32)],
        scratch_shapes=[pltpu.VMEM((B_H, B_DK, B_DK), F32)],
        compiler_params=_cparams(("arbitrary",)),
        name="mix0_prompt",
    )(sdec, p, lng, lnb, wm, bs, cos, sin, dec, cd, kd, rg)


def _mix0s_kernel(sdec_ref, p_ref, lng_ref, lnb_ref, cw_ref, bs_ref, cos_ref, sin_ref, dm_ref,
                  cd_ref, kd_ref, rg_ref, st_ref, mix_ref, gv_ref, so_ref):
    rows = 2 * DEC_S

    def shift_rows(t, d):
        return t if d == 0 else pltpu.roll(t, d, 0)

    u = _gelu(p_ref[:, 0:A_W])
    v = _gelu(p_ref[:, A_W:2 * A_W])
    mu = jnp.mean(v, axis=-1, keepdims=True)
    vc = v - mu
    v = vc * lax.rsqrt(jnp.mean(vc * vc, axis=-1, keepdims=True) + EPS) * lng_ref[...] + lnb_ref[...]
    gv_ref[...] = v
    sv = bs_ref[...]
    for d in range(DEC_S):
        sv = sv + cw_ref[d] * shift_rows(v, d)
    mix_ref[:, 0:A_W] = u * sv

    cos, sin = cos_ref[...], sin_ref[...]
    half = B_DK // 2

    def rope(t):
        t1, t2 = t[:, :half], t[:, half:]
        return jnp.concatenate([t1 * cos - t2 * sin, t2 * cos + t1 * sin], axis=1)

    grp = lax.shift_right_logical(lax.broadcasted_iota(I32, (rows, B_DK), 0), 2)
    base = 2 * A_W
    hw = B_H * B_DK
    for hd in range(B_H):
        sl = slice(hd * B_DK, (hd + 1) * B_DK)
        q = rope(p_ref[:, base + hd * B_DK: base + (hd + 1) * B_DK])
        k = rope(p_ref[:, base + hw + hd * B_DK: base + hw + (hd + 1) * B_DK]) * (B_DK ** -0.5)
        vr = p_ref[:, base + 2 * hw + hd * B_DK: base + 2 * hw + (hd + 1) * B_DK]
        g = p_ref[:, base + 3 * hw + hd * B_DK: base + 3 * hw + (hd + 1) * B_DK]
        o = jnp.zeros((rows, B_DK), F32)
        for d in range(DEC_S):
            s = jnp.sum(q * shift_rows(k, d), axis=-1, keepdims=True)
            o = o + (s * dm_ref[d][:, sl]) * shift_rows(vr, d)
        kd = k * kd_ref[:, sl]
        for bi in range(2):
            st = st_ref[bi, hd]
            mine = grp == bi
            o = o + jnp.where(mine, _dot(q, st) * cd_ref[:, sl], 0.0)
            so_ref[bi, hd] = sdec_ref[hd] * st + _dot_tn(jnp.where(mine, kd, 0.0), vr)
        on = o * lax.rsqrt(jnp.mean(o * o, axis=-1, keepdims=True) + EPS) * rg_ref[:, sl]
        mix_ref[:, A_W + hd * B_DK: A_W + (hd + 1) * B_DK] = (g * _sigmoid(g)) * on


def _mix0s_call(p, st_in, lng, lnb, cw, bs, cos, sin, dm, cd, kd, rg, sdec):
    rows = 2 * DEC_S
    nb = st_in.shape[0]
    full = lambda shape: pl.BlockSpec(shape, lambda c: (0,) * len(shape))
    return pl.pallas_call(
        _mix0s_kernel,
        grid=(nb // 2,),
        in_specs=[
            pl.BlockSpec(memory_space=pltpu.SMEM),
            pl.BlockSpec((rows, p.shape[1]), lambda c: (c, 0)),
            full((1, A_W)), full((1, A_W)),
            full((DEC_S, rows, A_W)), full((rows, A_W)),
            full((rows, LANES)), full((rows, LANES)),
            full((DEC_S, rows, B_H * B_DK)), full((rows, B_H * B_DK)), full((rows, B_H * B_DK)),
            full((1, B_H * B_DK)),
            pl.BlockSpec((2, B_H, B_DK, B_DK), lambda c: (c, 0, 0, 0)),
        ],
        out_specs=[
            pl.BlockSpec((rows, D), lambda c: (c, 0)),
            pl.BlockSpec((rows, A_W), lambda c: (c, 0)),
            pl.BlockSpec((2, B_H, B_DK, B_DK), lambda c: (c, 0, 0, 0)),
        ],
        out_shape=[
            jax.ShapeDtypeStruct((nb * DEC_S, D), F32),
            jax.ShapeDtypeStruct((nb * DEC_S, A_W), F32),
            jax.ShapeDtypeStruct(st_in.shape, F32),
        ],
        compiler_params=_cparams(("parallel",)),
        name="mix0_sample",
    )(sdec, p, lng, lnb, cw, bs, cos, sin, dm, cd, kd, rg, st_in)


def _outproj_kernel(mix_ref, w_ref, x_ref, gt_ref, g2_ref, sc_ref, sh_ref, wrh_ref, wrl_ref,
                    xo_ref, h_ref, lg_ref):
    y = _dot(mix_ref[...].astype(BF16), w_ref[...])
    xn = x_ref[...] + gt_ref[...] * y
    xo_ref[...] = xn
    h = _rms_mod(xn, g2_ref[...], sc_ref[...], sh_ref[...])
    h_ref[...] = h
    hi = h.astype(BF16)
    lo = (h - hi.astype(F32)).astype(BF16)
    lg_ref[...] = _dot(hi, wrh_ref[...]) + (_dot(lo, wrh_ref[...]) + _dot(hi, wrl_ref[...]))


def _outproj_call(mix, w, x, gate, g2, sc, sh, wr, *, tm, name):
    m = x.shape[0]
    wr_hi = wr.astype(BF16)
    wr_lo = (wr - wr_hi.astype(F32)).astype(BF16)

    def rows(a):
        if a.shape[0] == 1:
            return pl.BlockSpec((1, D), lambda i: (0, 0))
        return pl.BlockSpec((tm, D), lambda i: (i, 0))

    blk = pl.BlockSpec((tm, D), lambda i: (i, 0))
    return pl.pallas_call(
        _outproj_kernel,
        grid=(m // tm,),
        in_specs=[blk, pl.BlockSpec((D, D), lambda i: (0, 0)), blk, rows(gate),
                  pl.BlockSpec((1, D), lambda i: (0, 0)), rows(sc), rows(sh),
                  pl.BlockSpec((D, LANES), lambda i: (0, 0)), pl.BlockSpec((D, LANES), lambda i: (0, 0))],
        out_specs=[blk, blk, pl.BlockSpec((tm, LANES), lambda i: (i, 0))],
        out_shape=[jax.ShapeDtypeStruct((m, D), F32), jax.ShapeDtypeStruct((m, D), F32),
                   jax.ShapeDtypeStruct((m, LANES), F32)],
        compiler_params=_cparams(("parallel",)),
        name=name,
    )(mix, w, x, gate, g2, sc, sh, wr_hi, wr_lo)


def _route_kernel(lg_ref, bias_ref, tri_ref, eid_ref, gw_ref, rank_ref, cnt_ref, carry):
    i = pl.program_id(0)

    @pl.when(i == 0)
    def _():
        carry[...] = jnp.zeros_like(carry)

    tb = lg_ref.shape[1]
    s = _sigmoid(lg_ref[...])
    sel = s + bias_ref[...]
    epg = N_E // N_G
    r_sel = [sel[e:e + 1, :] for e in range(N_E)]
    r_s = [s[e:e + 1, :] for e in range(N_E)]

    gs = []
    for g in range(N_G):
        a, b, c, d = r_sel[epg * g: epg * g + 4]
        hi01, lo01 = jnp.maximum(a, b), jnp.minimum(a, b)
        hi23, lo23 = jnp.maximum(c, d), jnp.minimum(c, d)
        top1 = jnp.maximum(hi01, hi23)
        top2 = jnp.maximum(jnp.minimum(hi01, hi23), jnp.where(hi01 >= hi23, lo01, lo23))
        gs.append(top1 + top2)
    best = gs[0]
    gidx = jnp.zeros((1, tb), I32)
    for g in range(1, N_G):
        better = gs[g] > best
        gidx = jnp.where(better, g, gidx)
        best = jnp.where(better, gs[g], best)

    def pick_group(rws, j):
        out = rws[(N_G - 1) * epg + j]
        for g in range(N_G - 2, -1, -1):
            out = jnp.where(gidx == g, rws[g * epg + j], out)
        return out

    ing = [pick_group(r_sel, j) for j in range(epg)]
    sg = [pick_group(r_s, j) for j in range(epg)]

    def argmax_first(vals):
        bv, bi = vals[0], jnp.zeros((1, tb), I32)
        for j in range(1, epg):
            better = vals[j] > bv
            bi = jnp.where(better, j, bi)
            bv = jnp.where(better, vals[j], bv)
        return bi

    i1 = argmax_first(ing)
    i2 = argmax_first([jnp.where(i1 == j, -jnp.inf, ing[j]) for j in range(epg)])

    def pick_local(vals, idx):
        out = vals[epg - 1]
        for j in range(epg - 2, -1, -1):
            out = jnp.where(idx == j, vals[j], out)
        return out

    g1, g2 = pick_local(sg, i1), pick_local(sg, i2)
    den = g1 + g2
    e1 = gidx * epg + i1
    e2 = gidx * epg + i2
    eid_ref[0:1, :] = e1
    eid_ref[1:2, :] = e2
    gw_ref[0:1, :] = g1 / den
    gw_ref[1:2, :] = g2 / den

    eio = lax.broadcasted_iota(I32, (N_E, tb), 0)
    oh1 = eio == e1
    oh2 = eio == e2
    member = jnp.where(oh1, 1.0, jnp.where(oh2, 1.0, 0.0))
    before = carry[:, 0:1] + _dot(member.astype(BF16), tri_ref[...])
    rank_ref[0:1, :] = jnp.sum(jnp.where(oh1, before, 0.0), axis=0, keepdims=True).astype(I32)
    rank_ref[1:2, :] = jnp.sum(jnp.where(oh2, before, 0.0), axis=0, keepdims=True).astype(I32)
    carry[...] = carry[...] + jnp.sum(member, axis=1, keepdims=True)
    cnt_ref[...] = carry[...].astype(I32)


def _route_call(logits_t, bias, *, tb):
    t = logits_t.shape[1]
    tri = (jnp.arange(tb)[:, None] < jnp.arange(tb)[None, :]).astype(BF16)
    tok = pl.BlockSpec((2, tb), lambda i: (0, i))
    return pl.pallas_call(
        _route_kernel,
        grid=(t // tb,),
        in_specs=[pl.BlockSpec((N_E, tb), lambda i: (0, i)), pl.BlockSpec((N_E, 1), lambda i: (0, 0)),
                  pl.BlockSpec((tb, tb), lambda i: (0, 0))],
        out_specs=[tok, tok, tok, pl.BlockSpec((N_E, LANES), lambda i: (0, 0))],
        out_shape=[jax.ShapeDtypeStruct((2, t), I32), jax.ShapeDtypeStruct((2, t), F32),
                   jax.ShapeDtypeStruct((2, t), I32), jax.ShapeDtypeStruct((N_E, LANES), I32)],
        scratch_shapes=[pltpu.VMEM((N_E, LANES), F32)],
        compiler_params=_cparams(("arbitrary",)),
        name="route",
    )(logits_t, bias.reshape(N_E, 1), tri)


def _row_copy(src_ref, src_row, dst_ref, dst_row, sem):
    return pltpu.make_async_copy(src_ref.at[pl.ds(src_row, 1)], dst_ref.at[pl.ds(dst_row, 1)], sem)


def _dispatch_kernel(tok_ref, h_ref, o_ref, sem, *, tg):
    base = pl.program_id(0) * tg

    def issue(r, c):
        _row_copy(h_ref, tok_ref[base + r], o_ref, r, sem).start()
        return c

    lax.fori_loop(0, tg, issue, 0, unroll=8)
    pltpu.make_async_copy(h_ref.at[pl.ds(0, tg)], o_ref, sem).wait()


def _dispatch_call(slot_token, h, *, tg):
    n_slots = slot_token.shape[0]
    return pl.pallas_call(
        functools.partial(_dispatch_kernel, tg=tg),
        grid_spec=pltpu.PrefetchScalarGridSpec(
            num_scalar_prefetch=1,
            grid=(n_slots // tg,),
            in_specs=[pl.BlockSpec(memory_space=pl.ANY)],
            out_specs=pl.BlockSpec((tg, D), lambda i, tok: (i, 0)),
            scratch_shapes=[pltpu.SemaphoreType.DMA(())],
        ),
        out_shape=jax.ShapeDtypeStruct((n_slots, D), F32),
        compiler_params=_cparams(("arbitrary",)),
        name="moe_dispatch",
    )(slot_token, h)


def _gateup_kernel(ie_ref, in_ref, im_ref, iv_ref, x_ref, wg_ref, wu_ref, o_ref):
    @pl.when(iv_ref[pl.program_id(0)] == 1)
    def _():
        x = x_ref[...]
        g = _dot(x, wg_ref[0, 0])
        u = _dot(x, wu_ref[0, 0])
        o_ref[...] = ((g * _sigmoid(g)) * u).astype(BF16)

    @pl.when(iv_ref[pl.program_id(0)] == 0)
    def _():
        o_ref[...] = jnp.zeros_like(o_ref)


def _down_kernel(ie_ref, in_ref, im_ref, iv_ref, h_ref, wd_ref, o_ref):
    @pl.when(iv_ref[pl.program_id(0)] == 1)
    def _():
        o_ref[...] = lax.dot_general(h_ref[...], wd_ref[0, 0], (((1,), (0,)), ((), ())),
                                     preferred_element_type=F32)

    @pl.when(iv_ref[pl.program_id(0)] == 0)
    def _():
        o_ref[...] = jnp.zeros_like(o_ref)


def _gateup_call(items, x_sorted, wg, wu, layer, *, tm, tn):
    n_slots = x_sorted.shape[0]
    n_items = items[0].shape[0]
    w_spec = pl.BlockSpec((1, 1, D, tn), lambda i, ie, inn, im, iv: (layer, ie[i], 0, inn[i]))
    return pl.pallas_call(
        _gateup_kernel,
        grid_spec=pltpu.PrefetchScalarGridSpec(
            num_scalar_prefetch=4,
            grid=(n_items,),
            in_specs=[pl.BlockSpec((tm, D), lambda i, ie, inn, im, iv: (im[i], 0)), w_spec, w_spec],
            out_specs=pl.BlockSpec((tm, tn), lambda i, ie, inn, im, iv: (im[i], inn[i])),
        ),
        out_shape=jax.ShapeDtypeStruct((n_slots, FF), BF16),
        compiler_params=_cparams(("arbitrary",)),
        name="moe_gateup",
    )(*items, x_sorted, wg, wu)


def _down_call(items, hs, wd, layer, *, tm, tn):
    n_slots = hs.shape[0]
    n_items = items[0].shape[0]
    return pl.pallas_call(
        _down_kernel,
        grid_spec=pltpu.PrefetchScalarGridSpec(
            num_scalar_prefetch=4,
            grid=(n_items,),
            in_specs=[pl.BlockSpec((tm, FF), lambda i, ie, inn, im, iv: (im[i], 0)),
                      pl.BlockSpec((1, 1, FF, tn), lambda i, ie, inn, im, iv: (layer, ie[i], 0, inn[i]))],
            out_specs=pl.BlockSpec((tm, tn), lambda i, ie, inn, im, iv: (im[i], inn[i])),
        ),
        out_shape=jax.ShapeDtypeStruct((n_slots, D), F32),
        compiler_params=_cparams(("arbitrary",)),
        name="moe_down",
    )(*items, hs, wd)


def _moe_items(blocks_per_expert, n_tiles, n_blocks):
    bends = jnp.cumsum(blocks_per_expert)
    bstart = bends - blocks_per_expert
    total = bends[-1] * n_tiles
    step = jnp.arange(n_blocks * n_tiles, dtype=I32)
    j = jnp.minimum(step, total - 1)
    e = jnp.minimum(jnp.sum((j[:, None] >= (bends * n_tiles)[None, :]).astype(I32), axis=1), N_E - 1)
    local = j - bstart[e] * n_tiles
    nb = jnp.maximum(blocks_per_expert[e], 1)
    valid = step < total
    tail = jnp.maximum(step - total, 0)
    col = jnp.where(valid, local // nb, tail % n_tiles)
    row = jnp.where(valid, bstart[e] + local % nb, bends[-1] + tail // n_tiles)
    return e, col.astype(I32), row.astype(I32), valid.astype(I32)


def _combine_kernel(dest_ref, y_ref, x_ref, gm_ref, gw_ref, o_ref, buf, sem, *, tb, t_off, t_total):
    base = t_off + pl.program_id(0) * tb

    def issue(r, c):
        for k in range(2):
            _row_copy(y_ref, dest_ref[k * t_total + base + r], buf.at[k], r, sem).start()
        return c

    lax.fori_loop(0, tb, issue, 0, unroll=4)
    for k in range(2):
        pltpu.make_async_copy(y_ref.at[pl.ds(0, tb)], buf.at[k], sem).wait()
    gw = gw_ref[...]
    o_ref[...] = x_ref[...] + gm_ref[...] * (gw[:, 0:1] * buf[0] + gw[:, 1:2] * buf[1])


def _combine_call(dest_flat, y, x, gm, gw, *, tb, t_off, t_total, name):
    m = x.shape[0]
    if gm.shape[0] == 1:
        gm_spec = pl.BlockSpec((1, D), lambda i, d: (0, 0))
    else:
        gm_spec = pl.BlockSpec((tb, D), lambda i, d: (i, 0))
    return pl.pallas_call(
        functools.partial(_combine_kernel, tb=tb, t_off=t_off, t_total=t_total),
        grid_spec=pltpu.PrefetchScalarGridSpec(
            num_scalar_prefetch=1,
            grid=(m // tb,),
            in_specs=[pl.BlockSpec(memory_space=pl.ANY),
                      pl.BlockSpec((tb, D), lambda i, d: (i, 0)), gm_spec,
                      pl.BlockSpec((tb, 2), lambda i, d: (i, 0))],
            out_specs=pl.BlockSpec((tb, D), lambda i, d: (i, 0)),
            scratch_shapes=[pltpu.VMEM((2, tb, D), F32), pltpu.SemaphoreType.DMA(())],
        ),
        out_shape=jax.ShapeDtypeStruct((m, D), F32),
        compiler_params=_cparams(("arbitrary",)),
        name=name,
    )(dest_flat, y, x, gm, gw)


def _moe(layer, h_all, logits_all, w_router_bias, wg, wu, wd, xp, xs, gmp, gms):
    t = h_all.shape[0]
    tm = MOE_TM
    eid, gw, rank, cnt = _route_call(logits_all[:, :N_E].T, w_router_bias, tb=640)
    counts = cnt[:, 0]
    blocks_per_expert = (counts + tm - 1) // tm
    pstart = (jnp.cumsum(blocks_per_expert) - blocks_per_expert) * tm
    dest = pstart[eid] + rank
    n_blocks = (2 * t) // tm + N_E
    n_slots = n_blocks * tm
    tok = jnp.tile(jnp.arange(t, dtype=I32), 2)
    slot_token = jnp.zeros((n_slots,), I32).at[dest.reshape(-1)].set(tok)
    x_sorted = _dispatch_call(slot_token, h_all, tg=MOE_DISPATCH_ROWS)
    items1 = _moe_items(blocks_per_expert, FF // MOE_FF_TN, n_blocks)
    hs = _gateup_call(items1, x_sorted, wg, wu, layer, tm=tm, tn=MOE_FF_TN)
    items2 = _moe_items(blocks_per_expert, D // MOE_D_TN, n_blocks)
    y = _down_call(items2, hs, wd, layer, tm=tm, tn=MOE_D_TN)
    dest_flat = dest.reshape(-1)
    gwt = gw.T
    np_ = xp.shape[0]
    xp2 = _combine_call(dest_flat, y, xp, gmp, gwt[:np_], tb=256, t_off=0, t_total=t, name="moe_combine_p")
    xs2 = _combine_call(dest_flat, y, xs, gms, gwt[np_:], tb=128, t_off=np_, t_total=t, name="moe_combine_s")
    return xp2, xs2


def _attn_kernel(lam_ref, qt_ref, k_ref, vt_ref, g_ref, o_ref, m_scr, l_scr, acc_scr, *, tq, out_scale):
    qi = pl.program_id(1)
    qt = qt_ref[0, 0]
    feat = lax.broadcasted_iota(I32, (C_VD, tq), 0)
    zero = jnp.zeros_like(qt)
    qq = jnp.concatenate([jnp.where(feat < C_HD, qt, zero), jnp.where(feat >= C_HD, qt, zero)], axis=1)
    m_scr[...] = jnp.full_like(m_scr, -jnp.inf)
    l_scr[...] = jnp.zeros_like(l_scr)
    acc_scr[...] = jnp.zeros_like(acc_scr)

    def step(j, masked):
        off = pl.multiple_of(j * tq, tq)
        st = _dot(k_ref[pl.ds(off, tq), :], qq)
        if masked:
            key = lax.broadcasted_iota(I32, (tq, 2 * tq), 0)
            col = lax.broadcasted_iota(I32, (tq, 2 * tq), 1)
            st = jnp.where(key <= jnp.where(col >= tq, col - tq, col), st, -jnp.inf)
        m_prev = m_scr[...]
        m_new = jnp.maximum(m_prev, jnp.max(st, axis=0, keepdims=True))
        alpha = jnp.exp(m_prev - m_new)
        p = jnp.exp(st - m_new[0:1, :])
        l_scr[...] = alpha * l_scr[...] + jnp.sum(p, axis=0, keepdims=True)
        acc_scr[...] = alpha[0:1, :] * acc_scr[...] + _dot(vt_ref[0, j], p.astype(BF16))
        m_scr[...] = m_new

    def body(j, c):
        step(j, False)
        return c

    lax.fori_loop(0, qi, body, 0)
    step(qi, True)
    on = acc_scr[...] * (1.0 / l_scr[0:1, :])
    o = (on[:, 0:tq] - lam_ref[0] * on[:, tq:2 * tq]).T
    o = o * lax.rsqrt(jnp.mean(o * o, axis=-1, keepdims=True) + EPS) * g_ref[...] * out_scale
    o_ref[...] = o.astype(o_ref.dtype)


def _attn_call(lam, qt, k, vt, g, *, tq, out_scale):
    s = k.shape[0]
    return pl.pallas_call(
        functools.partial(_attn_kernel, tq=tq, out_scale=out_scale),
        grid=(C_H, s // tq),
        in_specs=[
            pl.BlockSpec(memory_space=pltpu.SMEM),
            pl.BlockSpec((1, 1, C_VD, tq), lambda h, i: (h, i, 0, 0)),
            pl.BlockSpec((s, C_VD), lambda h, i: (0, h)),
            pl.BlockSpec((1, s // tq, C_VD, tq), lambda h, i: (h, 0, 0, 0)),
            pl.BlockSpec((1, C_VD), lambda h, i: (0, 0)),
        ],
        out_specs=pl.BlockSpec((tq, C_VD), lambda h, i: (i, h)),
        out_shape=jax.ShapeDtypeStruct((s, C_H * C_VD), BF16),
        scratch_shapes=[pltpu.VMEM((SUBLANES, 2 * tq), F32), pltpu.VMEM((SUBLANES, 2 * tq), F32),
                        pltpu.VMEM((C_VD, 2 * tq), F32)],
        compiler_params=_cparams(("parallel", "arbitrary")),
        name="diff_attn_prompt",
    )(lam, qt, k, vt, g)


def _sattn_kernel(pt_ref, lam_ref, wq_ref, bias_ref, *rest, out_scale, n_pg):
    k_refs, v_refs = rest[:n_pg], rest[n_pg:2 * n_pg]
    kn_ref, vn_ref, g_ref, o_ref, m_scr, l_scr, acc_scr = rest[2 * n_pg:]
    b = pl.program_id(0)
    p = pl.program_id(1)

    @pl.when(p == 0)
    def _():
        m_scr[...] = jnp.full_like(m_scr, -jnp.inf)
        l_scr[...] = jnp.zeros_like(l_scr)
        acc_scr[...] = jnp.zeros_like(acc_scr)

    wq = wq_ref[0]
    reps = kn_ref.shape[0] // LANES

    def update(s, vmat):
        m_prev = m_scr[...]
        m_new = jnp.maximum(m_prev, jnp.max(s, axis=1, keepdims=True))
        alpha = jnp.exp(m_prev - m_new)
        pr = jnp.exp(s - jnp.concatenate([m_new] * reps, axis=1))
        l_scr[...] = alpha * l_scr[...] + jnp.sum(pr, axis=1, keepdims=True)
        acc_scr[...] = alpha * acc_scr[...] + _dot(pr, vmat)
        m_scr[...] = m_new

    for i in range(n_pg):
        update(_dot_nt(wq, k_refs[i][0]) + bias_ref[...], v_refs[i][0])

    @pl.when(p == pl.num_programs(1) - 1)
    def _():
        n = kn_ref.shape[0]
        rows = 2 * DEC_S
        s = _dot_nt(wq, kn_ref[...])
        c = lax.broadcasted_iota(I32, (wq.shape[0], n), 0)
        col = lax.broadcasted_iota(I32, (wq.shape[0], n), 1)
        tok = lax.shift_right_logical(col, 4)
        ok = ((col & (C_H - 1)) == lax.shift_right_logical(c, 3)) \
            & (lax.shift_right_logical(tok, 2) == b) & ((tok & (DEC_S - 1)) <= (c & (DEC_S - 1)))
        update(jnp.where(ok, s, -jnp.inf), vn_ref[...])
        out = acc_scr[...] / l_scr[...]
        for h in range(C_H):
            blk = out[h * rows:(h + 1) * rows, :]
            o = blk[0:DEC_S, :] - lam_ref[0] * blk[DEC_S:rows, :]
            on = o * lax.rsqrt(jnp.mean(o * o, axis=-1, keepdims=True) + EPS) * g_ref[...] * out_scale
            o_ref[0, :, h * C_VD:(h + 1) * C_VD] = on


def _sattn_call(page_table, lam, wq, bias, ck, cv, kn, vn, g, *, out_scale, n_pg):
    nb, n_pages = page_table.shape
    nrow = wq.shape[1]
    prow = ck.shape[1]

    def page_spec(i):
        return pl.BlockSpec((1, prow, C_VD), lambda b, p, pt: (pt[b * n_pages + p * n_pg + i], 0, 0))

    const2 = lambda b, p, pt: (0, 0)
    return pl.pallas_call(
        functools.partial(_sattn_kernel, out_scale=out_scale, n_pg=n_pg),
        grid_spec=pltpu.PrefetchScalarGridSpec(
            num_scalar_prefetch=1,
            grid=(nb, n_pages // n_pg),
            in_specs=[
                pl.BlockSpec(memory_space=pltpu.SMEM),
                pl.BlockSpec((1, nrow, C_VD), lambda b, p, pt: (b, 0, 0)),
                pl.BlockSpec(bias.shape, const2),
                *[page_spec(i) for i in range(n_pg)],
                *[page_spec(i) for i in range(n_pg)],
                pl.BlockSpec(kn.shape, const2),
                pl.BlockSpec(vn.shape, const2),
                pl.BlockSpec((1, C_VD), const2),
            ],
            out_specs=pl.BlockSpec((1, DEC_S, C_H * C_VD), lambda b, p, pt: (b, 0, 0)),
            scratch_shapes=[pltpu.VMEM((nrow, LANES), F32), pltpu.VMEM((nrow, LANES), F32),
                            pltpu.VMEM((nrow, C_VD), F32)],
        ),
        out_shape=jax.ShapeDtypeStruct((nb, DEC_S, C_H * C_VD), F32),
        compiler_params=_cparams(("parallel", "arbitrary")),
        name="diff_attn_sample",
    )(page_table.reshape(-1), lam, wq, bias, *([ck] * n_pg), *([cv] * n_pg), kn, vn, g)


def _rope0_tables(pos):
    half = B_DK // 2
    freqs = 10000.0 ** (-jnp.arange(half, dtype=F32) / half)
    ang = pos.astype(F32)[:, None] * freqs[None, :]
    return jnp.cos(ang), jnp.sin(ang)


def _rope1_tables(pos):
    half = C_ROT // 2
    freqs = 500000.0 ** (-jnp.arange(half, dtype=F32) / half)
    ang = pos.astype(F32)[:, None] * freqs[None, :]
    cos, sin = jnp.cos(ang), jnp.sin(ang)
    n = pos.shape[0]
    pad = C_HD - C_ROT
    cos_t = jnp.concatenate([cos, cos, jnp.ones((n, pad), F32)], axis=1)
    sin_a = jnp.concatenate([-sin, jnp.zeros((n, half + pad), F32)], axis=1)
    sin_b = jnp.concatenate([jnp.zeros((n, half), F32), sin, jnp.zeros((n, pad), F32)], axis=1)
    rep = lambda t: jnp.tile(t, (1, LANES // C_HD))
    return rep(cos_t), rep(sin_a), rep(sin_b)


def _log_gamma():
    return jnp.log(1.0 - 2.0 ** (-5.0 - jnp.arange(B_H, dtype=F32)))


def _lanes_per_head(per_head):
    return jnp.repeat(per_head, B_DK, axis=1)


def kernel(x_prompt, x_sample, c_prompt, c_sample, state_ret, cache_k, cache_v, page_table, w_ada, b_ada, norm_mix, norm_ffn, w_in0, gmlp_ln_g, gmlp_ln_b, gmlp_w_s, gmlp_b_s, ret_norm_g, w_out0, w_in1, q_norm_g, k_norm_g, lambda_q1, lambda_k1, lambda_q2, lambda_k2, subln_g, w_out1, w_router, router_bias, w_gate, w_up, w_down):
    xp = x_prompt.reshape(SEQ, D)
    xs = x_sample.reshape(NS, D)
    pos_p = jnp.arange(SEQ)
    pos_s = PAST + jnp.arange(DEC_S)

    mc = 1 + DEC_B
    mc_pad = -(-mc // SUBLANES) * SUBLANES
    c_all = jnp.concatenate([c_prompt, c_sample, jnp.zeros((mc_pad - mc, D), F32)], axis=0)
    mods = _ada_call(c_all, w_ada, b_ada).reshape(2, mc_pad, 6, D)
    wr_pad = jnp.pad(w_router, ((0, 0), (0, LANES - N_E)))

    def mod(l, i):
        return mods[l, 0:1, i], jnp.repeat(mods[l, 1:mc, i], DEC_S, axis=0)

    l = 0
    (shp, shs), (scp, scs), (gtp, gts) = mod(l, 0), mod(l, 1), mod(l, 2)
    (sh2p, sh2s), (sc2p, sc2s), (gt2p, gt2s) = mod(l, 3), mod(l, 4), mod(l, 5)
    gmix = norm_mix[l].reshape(1, D)
    w_in = w_in0[0].astype(BF16)
    pp = _inproj_call(xp, gmix, scp, shp, w_in, mode="f32", tm=512, tn=1024, name="inproj0_p")[0]
    ps = _inproj_call(xs, gmix, scs, shs, w_in, mode="f32", tm=NS, tn=1024, name="inproj0_s")[0]

    log_g = _log_gamma()
    lng, lnb = gmlp_ln_g[0].reshape(1, A_W), gmlp_ln_b[0].reshape(1, A_W)
    rg = ret_norm_g[0].reshape(1, B_H * B_DK)
    ch = 128
    idx = jnp.arange(ch, dtype=F32)
    rel = idx[:, None] - idx[None, :]
    dec = jnp.where(rel[None] >= 0, jnp.exp(jnp.maximum(rel, 0.0)[None] * log_g[:, None, None]), 0.0)
    cd = _lanes_per_head(jnp.exp((idx + 1.0)[:, None] * log_g[None, :]))
    kd = _lanes_per_head(jnp.exp((ch - 1.0 - idx)[:, None] * log_g[None, :]))
    sdec = jnp.exp(ch * log_g)
    wm = jnp.where(jnp.tril(jnp.ones((ch, ch), dtype=bool))[None], gmlp_w_s[0], 0.0).astype(BF16)
    bs = jnp.repeat(gmlp_b_s[0].T, A_W // A_H, axis=1)
    cos0p, sin0p = _rope0_tables(pos_p)
    mix_p, ret_p = _mix0p_call(pp, lng, lnb, wm, bs, cos0p, sin0p, dec, cd, kd, rg, sdec)

    rows = 2 * DEC_S
    tloc = jnp.arange(rows) % DEC_S
    ws4 = gmlp_w_s[0][:, :DEC_S, :DEC_S]
    cw = []
    dm = []
    for d in range(DEC_S):
        src = tloc - d
        ok = src >= 0
        coef = jnp.where(ok[:, None], ws4[:, tloc, jnp.maximum(src, 0)].T, 0.0)
        cw.append(jnp.repeat(coef, A_W // A_H, axis=1))
        dm.append(_lanes_per_head(jnp.where(ok[:, None], jnp.exp(float(d) * log_g)[None, :], 0.0)))
    cw, dm = jnp.stack(cw), jnp.stack(dm)
    bs_s = jnp.repeat(gmlp_b_s[0][:, :DEC_S].T[tloc], A_W // A_H, axis=1)
    tl = tloc.astype(F32)
    cd_s = _lanes_per_head(jnp.exp((tl + 1.0)[:, None] * log_g[None, :]))
    kd_s = _lanes_per_head(jnp.exp((DEC_S - 1.0 - tl)[:, None] * log_g[None, :]))
    sdec_s = jnp.exp(DEC_S * log_g)
    cos0s, sin0s = _rope0_tables(pos_s)
    mix_s, gv_s, ret_s = _mix0s_call(ps, state_ret[0], lng, lnb, cw, bs_s, cos0s[tloc], sin0s[tloc],
                                     dm, cd_s, kd_s, rg, sdec_s)

    gffn = norm_ffn[l].reshape(1, D)
    w_out = w_out0[0].astype(BF16)
    xp, h2p, lgp = _outproj_call(mix_p, w_out, xp, gtp, gffn, sc2p, sh2p, wr_pad, tm=256, name="outproj0_p")
    xs, h2s, lgs = _outproj_call(mix_s, w_out, xs, gts, gffn, sc2s, sh2s, wr_pad, tm=NS, name="outproj0_s")
    xp, xs = _moe(l, jnp.concatenate([h2p, h2s], axis=0), jnp.concatenate([lgp, lgs], axis=0), router_bias,
                  w_gate, w_up, w_down, xp, xs, gt2p, gt2s)

    l = 1
    (shp, shs), (scp, scs), (gtp, gts) = mod(l, 0), mod(l, 1), mod(l, 2)
    (sh2p, sh2s), (sc2p, sc2s), (gt2p, gt2s) = mod(l, 3), mod(l, 4), mod(l, 5)
    gmix = norm_mix[l].reshape(1, D)
    lambda_init = 0.8 - 0.6 * math.exp(-0.3 * l)
    lam = (jnp.exp(jnp.sum(lambda_q1[0] * lambda_k1[0])) - jnp.exp(jnp.sum(lambda_q2[0] * lambda_k2[0]))
           + lambda_init).reshape(1).astype(F32)
    hw = C_H * C_VD
    wq_w = w_in1[0][:, 0:hw].astype(BF16)
    wk_w = w_in1[0][:, hw:2 * hw].astype(BF16)
    wv_w = w_in1[0][:, 2 * hw:3 * hw].astype(BF16)
    gq = jnp.tile(q_norm_g[0], LANES // C_HD).reshape(1, LANES)
    gk = jnp.tile(k_norm_g[0], LANES // C_HD).reshape(1, LANES)
    lane = jnp.arange(LANES)
    bd = (lane[:, None] // C_HD == lane[None, :] // C_HD).astype(BF16)
    tabs_p = _rope1_tables(pos_p)
    tabs_s = tuple(jnp.tile(t, (DEC_B, 1)) for t in _rope1_tables(pos_s))
    qscale = C_HD ** -0.5

    q_p = _inproj_call(xp, gmix, scp, shp, wq_w, mode="q", tm=512, tn=1024, extras=(gq, *tabs_p, bd),
                       scale=qscale, name="inproj1_q_p")[0]
    k_p, k_pb = _inproj_call(xp, gmix, scp, shp, wk_w, mode="k", tm=512, tn=1024, extras=(gk, *tabs_p, bd),
                             name="inproj1_k_p")
    v_p, v_pb = _inproj_call(xp, gmix, scp, shp, wv_w, mode="dual", tm=512, tn=1024, name="inproj1_v_p")
    q_s = _inproj_call(xs, gmix, scs, shs, wq_w, mode="q", tm=NS, tn=1024, extras=(gq, *tabs_s, bd),
                       scale=qscale, name="inproj1_q_s")[0]
    k_s, k_sb = _inproj_call(xs, gmix, scs, shs, wk_w, mode="k", tm=NS, tn=1024, extras=(gk, *tabs_s, bd),
                             name="inproj1_k_s")
    v_s, v_sb = _inproj_call(xs, gmix, scs, shs, wv_w, mode="dual", tm=NS, tn=1024, name="inproj1_v_s")

    gsub = subln_g[0].reshape(1, C_VD)
    out_scale = 1.0 - lambda_init
    nqb = SEQ // ATT_TQ
    to_tiles = lambda t: t.reshape(nqb, ATT_TQ, C_H, C_VD).transpose(2, 0, 3, 1)
    att_p = _attn_call(lam, to_tiles(q_p), k_pb, to_tiles(v_pb), gsub, tq=ATT_TQ, out_scale=out_scale)

    q5 = q_s.astype(F32).reshape(DEC_B, DEC_S, C_H, 2, C_HD).transpose(0, 2, 3, 1, 4)
    wq_rows = (q5[:, :, :, :, None, :] * jnp.eye(2, dtype=F32)[None, None, :, None, :, None]
               ).reshape(DEC_B, C_H * 2 * DEC_S, C_VD)
    n_pool = cache_k.shape[1]
    prow = PAGE * C_H
    srow = jnp.arange(C_H * 2 * DEC_S)[:, None] // (2 * DEC_S)
    head_bias = jnp.where(jnp.arange(prow)[None, :] % C_H == srow, 0.0, -jnp.inf).astype(F32)
    att_s = _sattn_call(page_table, lam, wq_rows, head_bias,
                        cache_k.reshape(n_pool, prow, C_VD), cache_v.reshape(n_pool, prow, C_VD),
                        k_s.reshape(NS * C_H, C_VD), v_s.reshape(NS * C_H, C_VD), gsub,
                        out_scale=out_scale, n_pg=SATT_PAGES)
    att_s = att_s.reshape(NS, hw)

    gffn = norm_ffn[l].reshape(1, D)
    w_out = w_out1[0].astype(BF16)
    xp, h2p, lgp = _outproj_call(att_p, w_out, xp, gtp, gffn, sc2p, sh2p, wr_pad, tm=256, name="outproj1_p")
    xs, h2s, lgs = _outproj_call(att_s, w_out, xs, gts, gffn, sc2s, sh2s, wr_pad, tm=NS, name="outproj1_s")
    xp, xs = _moe(l, jnp.concatenate([h2p, h2s], axis=0), jnp.concatenate([lgp, lgs], axis=0), router_bias,
                  w_gate, w_up, w_down, xp, xs, gt2p, gt2s)

    return (xp.reshape(1, SEQ, D), xs.reshape(DEC_B, DEC_S, D),
            ret_p.reshape(1, 1, B_H, B_DK, B_DK), ret_s.reshape(1, DEC_B, B_H, B_DK, B_DK),
            gv_s.reshape(1, DEC_B, DEC_S, A_W),
            k_p.reshape(1, 1, SEQ, C_H, C_VD), v_p.reshape(1, 1, SEQ, C_H, C_VD),
            k_s.reshape(1, DEC_B, DEC_S, C_H, C_VD), v_s.reshape(1, DEC_B, DEC_S, C_H, C_VD))
```

```python
import functools
import math

import jax
import jax.numpy as jnp
from jax import lax
from jax.experimental import pallas as pl
from jax.experimental.pallas import tpu as pltpu

F32 = jnp.float32
BF16 = jnp.bfloat16
I32 = jnp.int32
HIGHEST = lax.Precision.HIGHEST

D = 2048
SEQ = 8192
DEC_B = 32
DEC_S = 4
NS = DEC_B * DEC_S
PAST = 8192
PAGE = 128
EPS = 1e-6
A_W = 1024
A_H = 8
B_H = 4
B_DK = 256
C_H = 16
C_HD = 64
C_VD = 128
C_ROT = 16
N_E = 16
N_G = 4
FF = 1536

LANES = 128
SUBLANES = 8
VMEM_LIMIT_BYTES = 56 * 1024 * 1024

MOE_TM = 256
MOE_FF_TN = 768
MOE_D_TN = 1024
MOE_DISPATCH_ROWS = 3 * MOE_TM
ATT_TQ = 512
SATT_PAGES = 4


def _cparams(sem):
    return pltpu.CompilerParams(dimension_semantics=sem, vmem_limit_bytes=VMEM_LIMIT_BYTES)


def _sigmoid(x):
    return 1.0 / (1.0 + jnp.exp(-x))


def _gelu(x):
    return x * (0.5 * (1.0 + jnp.tanh(0.7978845608028654 * (x + 0.044715 * (x * x * x)))))


def _rms_mod(x, g, sc, sh):
    ms = jnp.mean(x * x, axis=-1, keepdims=True)
    return (x * lax.rsqrt(ms + EPS)) * g * (1.0 + sc) + sh


def _dot(a, b):
    return jnp.dot(a, b, preferred_element_type=F32)


def _dot_nt(a, b):
    return lax.dot_general(a, b, (((1,), (1,)), ((), ())), preferred_element_type=F32)


def _dot_tn(a, b):
    return lax.dot_general(a, b, (((0,), (0,)), ((), ())), preferred_element_type=F32)


def _ada_kernel(c_ref, w_ref, b_ref, o_ref):
    c = c_ref[...]
    s = c * _sigmoid(c)
    o_ref[0] = jnp.dot(s, w_ref[0], preferred_element_type=F32, precision=HIGHEST) + b_ref[0]


def _ada_call(c_all, w_ada, b_ada):
    nl, _, n = w_ada.shape
    mc = c_all.shape[0]
    tn = 1024
    return pl.pallas_call(
        _ada_kernel,
        grid=(nl, n // tn),
        in_specs=[
            pl.BlockSpec((mc, D), lambda l, j: (0, 0)),
            pl.BlockSpec((1, D, tn), lambda l, j: (l, 0, j)),
            pl.BlockSpec((1, 1, tn), lambda l, j: (l, 0, j)),
        ],
        out_specs=pl.BlockSpec((1, mc, tn), lambda l, j: (l, 0, j)),
        out_shape=jax.ShapeDtypeStruct((nl, mc, n), F32),
        compiler_params=_cparams(("parallel", "parallel")),
        name="ada",
    )(c_all, w_ada, b_ada.reshape(nl, 1, n))


def _inproj_kernel(*refs, mode, scale, tn):
    x_ref, g_ref, sc_ref, sh_ref, w_ref = refs[:5]
    h_scr = refs[-1]

    @pl.when(pl.program_id(1) == 0)
    def _():
        h_scr[...] = _rms_mod(x_ref[...], g_ref[...], sc_ref[...], sh_ref[...]).astype(BF16)

    acc = _dot(h_scr[...], w_ref[...])
    if mode in ("q", "k"):
        gq_ref, cos_ref, sa_ref, sb_ref, bd_ref = refs[5:10]
        outs = refs[10:-1]
        gq = gq_ref[...]
        cos_t, sin_a, sin_b, bd = cos_ref[...], sa_ref[...], sb_ref[...], bd_ref[...]
        for j in range(tn // LANES):
            sl = slice(j * LANES, (j + 1) * LANES)
            y = acc[:, sl]
            sq = y * y
            hi = sq.astype(BF16)
            lo = (sq - hi.astype(F32)).astype(BF16)
            ss = _dot(hi, bd) + _dot(lo, bd)
            yn = y * lax.rsqrt(ss * (1.0 / C_HD) + EPS) * gq
            r = yn * cos_t + pltpu.roll(yn, LANES - C_ROT // 2, 1) * sin_a + pltpu.roll(yn, C_ROT // 2, 1) * sin_b
            if scale != 1.0:
                r = r * scale
            if mode == "q":
                outs[0][:, sl] = r.astype(BF16)
            else:
                outs[0][:, sl] = r
                outs[1][:, sl] = r.astype(BF16)
    elif mode == "dual":
        outs = refs[5:-1]
        outs[0][...] = acc
        outs[1][...] = acc.astype(BF16)
    else:
        refs[5][...] = acc


def _inproj_call(x, g, sc, sh, w, *, mode, tm, tn, extras=(), scale=1.0, name="inproj"):
    m = x.shape[0]
    n = w.shape[1]

    def rows(a):
        if a.shape[0] == 1:
            return pl.BlockSpec((1, D), lambda i, j: (0, 0))
        return pl.BlockSpec((tm, D), lambda i, j: (i, 0))

    in_specs = [
        pl.BlockSpec((tm, D), lambda i, j: (i, 0)),
        pl.BlockSpec((1, D), lambda i, j: (0, 0)),
        rows(sc),
        rows(sh),
        pl.BlockSpec((D, tn), lambda i, j: (0, j)),
    ]
    args = [x, g, sc, sh, w]
    o_spec = pl.BlockSpec((tm, tn), lambda i, j: (i, j))
    if mode in ("q", "k"):
        gq, cos_t, sin_a, sin_b, bd = extras
        in_specs += [
            pl.BlockSpec((1, LANES), lambda i, j: (0, 0)),
            pl.BlockSpec((tm, LANES), lambda i, j: (i, 0)),
            pl.BlockSpec((tm, LANES), lambda i, j: (i, 0)),
            pl.BlockSpec((tm, LANES), lambda i, j: (i, 0)),
            pl.BlockSpec((LANES, LANES), lambda i, j: (0, 0)),
        ]
        args += [gq, cos_t, sin_a, sin_b, bd]
    if mode == "q":
        out_shape = [jax.ShapeDtypeStruct((m, n), BF16)]
    elif mode in ("k", "dual"):
        out_shape = [jax.ShapeDtypeStruct((m, n), F32), jax.ShapeDtypeStruct((m, n), BF16)]
    else:
        out_shape = [jax.ShapeDtypeStruct((m, n), F32)]
    res = pl.pallas_call(
        functools.partial(_inproj_kernel, mode=mode, scale=scale, tn=tn),
        grid=(m // tm, n // tn),
        in_specs=in_specs,
        out_specs=[o_spec] * len(out_shape),
        out_shape=out_shape,
        scratch_shapes=[pltpu.VMEM((tm, D), BF16)],
        compiler_params=_cparams(("parallel", "arbitrary")),
        name=name,
    )(*args)
    return res


def _mix0p_kernel(sdec_ref, p_ref, lng_ref, lnb_ref, wm_ref, bs_ref, cos_ref, sin_ref, dec_ref,
                  cd_ref, kd_ref, rg_ref, mix_ref, so_ref, st_scr):
    c = pl.program_id(0)

    @pl.when(c == 0)
    def _():
        st_scr[...] = jnp.zeros_like(st_scr)

    u = _gelu(p_ref[:, 0:A_W])
    v = _gelu(p_ref[:, A_W:2 * A_W])
    mu = jnp.mean(v, axis=-1, keepdims=True)
    vc = v - mu
    v = vc * lax.rsqrt(jnp.mean(vc * vc, axis=-1, keepdims=True) + EPS) * lng_ref[...] + lnb_ref[...]
    vb = v.astype(BF16)
    for hh in range(A_H):
        sl = slice(hh * LANES, (hh + 1) * LANES)
        sv = _dot(wm_ref[hh], vb[:, sl]) + bs_ref[:, sl]
        mix_ref[:, sl] = (u[:, sl] * sv).astype(BF16)

    cos, sin = cos_ref[...], sin_ref[...]
    half = B_DK // 2

    def rope(t):
        t1, t2 = t[:, :half], t[:, half:]
        return jnp.concatenate([t1 * cos - t2 * sin, t2 * cos + t1 * sin], axis=1)

    base = 2 * A_W
    hw = B_H * B_DK
    for hd in range(B_H):
        sl = slice(hd * B_DK, (hd + 1) * B_DK)
        q = rope(p_ref[:, base + hd * B_DK: base + (hd + 1) * B_DK])
        k = rope(p_ref[:, base + hw + hd * B_DK: base + hw + (hd + 1) * B_DK]) * (B_DK ** -0.5)
        vr = p_ref[:, base + 2 * hw + hd * B_DK: base + 2 * hw + (hd + 1) * B_DK]
        g = p_ref[:, base + 3 * hw + hd * B_DK: base + 3 * hw + (hd + 1) * B_DK]
        qb, kb, vrb = q.astype(BF16), k.astype(BF16), vr.astype(BF16)
        inner = _dot_nt(qb, kb) * dec_ref[hd]
        st = st_scr[hd]
        o = _dot(inner.astype(BF16), vrb) + _dot(qb, st.astype(BF16)) * cd_ref[:, sl]
        kdb = (k * kd_ref[:, sl]).astype(BF16)
        st_scr[hd] = sdec_ref[hd] * st + _dot_tn(kdb, vrb)
        on = o * lax.rsqrt(jnp.mean(o * o, axis=-1, keepdims=True) + EPS) * rg_ref[:, sl]
        mix_ref[:, A_W + hd * B_DK: A_W + (hd + 1) * B_DK] = ((g * _sigmoid(g)) * on).astype(BF16)

    @pl.when(c == pl.num_programs(0) - 1)
    def _():
        so_ref[...] = st_scr[...]


def _mix0p_call(p, lng, lnb, wm, bs, cos, sin, dec, cd, kd, rg, sdec):
    m = p.shape[0]
    ch = 128
    full = lambda shape: pl.BlockSpec(shape, lambda c: (0,) * len(shape))
    return pl.pallas_call(
        _mix0p_kernel,
        grid=(m // ch,),
        in_specs=[
            pl.BlockSpec(memory_space=pltpu.SMEM),
            pl.BlockSpec((ch, p.shape[1]), lambda c: (c, 0)),
            full((1, A_W)), full((1, A_W)),
            full((A_H, ch, ch)), full((ch, A_W)),
            pl.BlockSpec((ch, LANES), lambda c: (c, 0)),
            pl.BlockSpec((ch, LANES), lambda c: (c, 0)),
            full((B_H, ch, ch)), full((ch, B_H * B_DK)), full((ch, B_H * B_DK)),
            full((1, B_H * B_DK)),
        ],
        out_specs=[
            pl.BlockSpec((ch, D), lambda c: (c, 0)),
            full((B_H, B_DK, B_DK)),
        ],
        out_shape=[jax.ShapeDtypeStruct((m, D), BF16), jax.ShapeDtypeStruct((B_H, B_DK, B_DK), F32)],
        scratch_shapes=[pltpu.VMEM((B_H, B_DK, B_DK), F32)],
        compiler_params=_cparams(("arbitrary",)),
        name="mix0_prompt",
    )(sdec, p, lng, lnb, wm, bs, cos, sin, dec, cd, kd, rg)


def _mix0s_kernel(sdec_ref, p_ref, lng_ref, lnb_ref, cw_ref, bs_ref, cos_ref, sin_ref, dm_ref,
                  cd_ref, kd_ref, rg_ref, st_ref, mix_ref, gv_ref, so_ref):
    rows = 2 * DEC_S

    def shift_rows(t, d):
        return t if d == 0 else pltpu.roll(t, d, 0)

    u = _gelu(p_ref[:, 0:A_W])
    v = _gelu(p_ref[:, A_W:2 * A_W])
    mu = jnp.mean(v, axis=-1, keepdims=True)
    vc = v - mu
    v = vc * lax.rsqrt(jnp.mean(vc * vc, axis=-1, keepdims=True) + EPS) * lng_ref[...] + lnb_ref[...]
    gv_ref[...] = v
    sv = bs_ref[...]
    for d in range(DEC_S):
        sv = sv + cw_ref[d] * shift_rows(v, d)
    mix_ref[:, 0:A_W] = u * sv

    cos, sin = cos_ref[...], sin_ref[...]
    half = B_DK // 2

    def rope(t):
        t1, t2 = t[:, :half], t[:, half:]
        return jnp.concatenate([t1 * cos - t2 * sin, t2 * cos + t1 * sin], axis=1)

    grp = lax.shift_right_logical(lax.broadcasted_iota(I32, (rows, B_DK), 0), 2)
    base = 2 * A_W
    hw = B_H * B_DK
    for hd in range(B_H):
        sl = slice(hd * B_DK, (hd + 1) * B_DK)
        q = rope(p_ref[:, base + hd * B_DK: base + (hd + 1) * B_DK])
        k = rope(p_ref[:, base + hw + hd * B_DK: base + hw + (hd + 1) * B_DK]) * (B_DK ** -0.5)
        vr = p_ref[:, base + 2 * hw + hd * B_DK: base + 2 * hw + (hd + 1) * B_DK]
        g = p_ref[:, base + 3 * hw + hd * B_DK: base + 3 * hw + (hd + 1) * B_DK]
        o = jnp.zeros((rows, B_DK), F32)
        for d in range(DEC_S):
            s = jnp.sum(q * shift_rows(k, d), axis=-1, keepdims=True)
            o = o + (s * dm_ref[d][:, sl]) * shift_rows(vr, d)
        kd = k * kd_ref[:, sl]
        for bi in range(2):
            st = st_ref[bi, hd]
            mine = grp == bi
            o = o + jnp.where(mine, _dot(q, st) * cd_ref[:, sl], 0.0)
            so_ref[bi, hd] = sdec_ref[hd] * st + _dot_tn(jnp.where(mine, kd, 0.0), vr)
        on = o * lax.rsqrt(jnp.mean(o * o, axis=-1, keepdims=True) + EPS) * rg_ref[:, sl]
        mix_ref[:, A_W + hd * B_DK: A_W + (hd + 1) * B_DK] = (g * _sigmoid(g)) * on


def _mix0s_call(p, st_in, lng, lnb, cw, bs, cos, sin, dm, cd, kd, rg, sdec):
    rows = 2 * DEC_S
    nb = st_in.shape[0]
    full = lambda shape: pl.BlockSpec(shape, lambda c: (0,) * len(shape))
    return pl.pallas_call(
        _mix0s_kernel,
        grid=(nb // 2,),
        in_specs=[
            pl.BlockSpec(memory_space=pltpu.SMEM),
            pl.BlockSpec((rows, p.shape[1]), lambda c: (c, 0)),
            full((1, A_W)), full((1, A_W)),
            full((DEC_S, rows, A_W)), full((rows, A_W)),
            full((rows, LANES)), full((rows, LANES)),
            full((DEC_S, rows, B_H * B_DK)), full((rows, B_H * B_DK)), full((rows, B_H * B_DK)),
            full((1, B_H * B_DK)),
            pl.BlockSpec((2, B_H, B_DK, B_DK), lambda c: (c, 0, 0, 0)),
        ],
        out_specs=[
            pl.BlockSpec((rows, D), lambda c: (c, 0)),
            pl.BlockSpec((rows, A_W), lambda c: (c, 0)),
            pl.BlockSpec((2, B_H, B_DK, B_DK), lambda c: (c, 0, 0, 0)),
        ],
        out_shape=[
            jax.ShapeDtypeStruct((nb * DEC_S, D), F32),
            jax.ShapeDtypeStruct((nb * DEC_S, A_W), F32),
            jax.ShapeDtypeStruct(st_in.shape, F32),
        ],
        compiler_params=_cparams(("parallel",)),
        name="mix0_sample",
    )(sdec, p, lng, lnb, cw, bs, cos, sin, dm, cd, kd, rg, st_in)


def _outproj_kernel(*refs):
    mix_ref, w_ref, x_ref, gt_ref, g2_ref, sc_ref, sh_ref, wrh_ref, wrl_ref, xo_ref, h_ref, lg_ref = refs
    y = _dot(mix_ref[...].astype(BF16), w_ref[...])
    xn = x_ref[...] + gt_ref[...] * y
    xo_ref[...] = xn
    h = _rms_mod(xn, g2_ref[...], sc_ref[...], sh_ref[...])
    h_ref[...] = h.reshape(h_ref.shape)
    hi = h.astype(BF16)
    lo = (h - hi.astype(F32)).astype(BF16)
    lg_ref[...] = _dot(hi, wrh_ref[...]) + (_dot(lo, wrh_ref[...]) + _dot(hi, wrl_ref[...]))


def _outproj_call(mix, w, x, gate, g2, sc, sh, wr, *, tm, name):
    m = x.shape[0]
    wr_hi = wr.astype(BF16)
    wr_lo = (wr - wr_hi.astype(F32)).astype(BF16)
    chunks = D // LANES

    def rows(a):
        if a.shape[0] == 1:
            return pl.BlockSpec((1, D), lambda i: (0, 0))
        return pl.BlockSpec((tm, D), lambda i: (i, 0))

    blk = pl.BlockSpec((tm, D), lambda i: (i, 0))
    in_specs = [blk, pl.BlockSpec((D, D), lambda i: (0, 0)), blk, rows(gate),
                pl.BlockSpec((1, D), lambda i: (0, 0)), rows(sc), rows(sh),
                pl.BlockSpec((D, LANES), lambda i: (0, 0)), pl.BlockSpec((D, LANES), lambda i: (0, 0))]
    return pl.pallas_call(
        _outproj_kernel,
        grid=(m // tm,),
        in_specs=in_specs,
        out_specs=[blk, pl.BlockSpec((tm, chunks, LANES), lambda i: (i, 0, 0)),
                   pl.BlockSpec((tm, LANES), lambda i: (i, 0))],
        out_shape=[jax.ShapeDtypeStruct((m, D), F32), jax.ShapeDtypeStruct((m, chunks, LANES), F32),
                   jax.ShapeDtypeStruct((m, LANES), F32)],
        compiler_params=_cparams(("parallel",)),
        name=name,
    )(mix, w, x, gate, g2, sc, sh, wr_hi, wr_lo)


def _route_kernel(lg_ref, bias_ref, tri_ref, eid_ref, gw_ref, rank_ref, cnt_ref, carry):
    i = pl.program_id(0)

    @pl.when(i == 0)
    def _():
        carry[...] = jnp.zeros_like(carry)

    tb = lg_ref.shape[1]
    s = _sigmoid(lg_ref[...])
    sel = s + bias_ref[...]
    epg = N_E // N_G
    r_sel = [sel[e:e + 1, :] for e in range(N_E)]
    r_s = [s[e:e + 1, :] for e in range(N_E)]

    gs = []
    for g in range(N_G):
        a, b, c, d = r_sel[epg * g: epg * g + 4]
        hi01, lo01 = jnp.maximum(a, b), jnp.minimum(a, b)
        hi23, lo23 = jnp.maximum(c, d), jnp.minimum(c, d)
        top1 = jnp.maximum(hi01, hi23)
        top2 = jnp.maximum(jnp.minimum(hi01, hi23), jnp.where(hi01 >= hi23, lo01, lo23))
        gs.append(top1 + top2)
    best = gs[0]
    gidx = jnp.zeros((1, tb), I32)
    for g in range(1, N_G):
        better = gs[g] > best
        gidx = jnp.where(better, g, gidx)
        best = jnp.where(better, gs[g], best)

    def pick_group(rws, j):
        out = rws[(N_G - 1) * epg + j]
        for g in range(N_G - 2, -1, -1):
            out = jnp.where(gidx == g, rws[g * epg + j], out)
        return out

    ing = [pick_group(r_sel, j) for j in range(epg)]
    sg = [pick_group(r_s, j) for j in range(epg)]

    def argmax_first(vals):
        bv, bi = vals[0], jnp.zeros((1, tb), I32)
        for j in range(1, epg):
            better = vals[j] > bv
            bi = jnp.where(better, j, bi)
            bv = jnp.where(better, vals[j], bv)
        return bi

    i1 = argmax_first(ing)
    i2 = argmax_first([jnp.where(i1 == j, -jnp.inf, ing[j]) for j in range(epg)])

    def pick_local(vals, idx):
        out = vals[epg - 1]
        for j in range(epg - 2, -1, -1):
            out = jnp.where(idx == j, vals[j], out)
        return out

    g1, g2 = pick_local(sg, i1), pick_local(sg, i2)
    den = g1 + g2
    e1 = gidx * epg + i1
    e2 = gidx * epg + i2
    eid_ref[0:1, :] = e1
    eid_ref[1:2, :] = e2
    gw_ref[0:1, :] = g1 / den
    gw_ref[1:2, :] = g2 / den

    eio = lax.broadcasted_iota(I32, (N_E, tb), 0)
    oh1 = eio == e1
    oh2 = eio == e2
    member = jnp.where(oh1, 1.0, jnp.where(oh2, 1.0, 0.0))
    before = carry[:, 0:1] + _dot(member.astype(BF16), tri_ref[...])
    rank_ref[0:1, :] = jnp.sum(jnp.where(oh1, before, 0.0), axis=0, keepdims=True).astype(I32)
    rank_ref[1:2, :] = jnp.sum(jnp.where(oh2, before, 0.0), axis=0, keepdims=True).astype(I32)
    carry[...] = carry[...] + jnp.sum(member, axis=1, keepdims=True)
    cnt_ref[...] = carry[...].astype(I32)


def _route_call(logits_t, bias, *, tb):
    t = logits_t.shape[1]
    tri = (jnp.arange(tb)[:, None] < jnp.arange(tb)[None, :]).astype(BF16)
    tok = pl.BlockSpec((2, tb), lambda i: (0, i))
    return pl.pallas_call(
        _route_kernel,
        grid=(t // tb,),
        in_specs=[pl.BlockSpec((N_E, tb), lambda i: (0, i)), pl.BlockSpec((N_E, 1), lambda i: (0, 0)),
                  pl.BlockSpec((tb, tb), lambda i: (0, 0))],
        out_specs=[tok, tok, tok, pl.BlockSpec((N_E, LANES), lambda i: (0, 0))],
        out_shape=[jax.ShapeDtypeStruct((2, t), I32), jax.ShapeDtypeStruct((2, t), F32),
                   jax.ShapeDtypeStruct((2, t), I32), jax.ShapeDtypeStruct((N_E, LANES), I32)],
        scratch_shapes=[pltpu.VMEM((N_E, LANES), F32)],
        compiler_params=_cparams(("arbitrary",)),
        name="route",
    )(logits_t, bias.reshape(N_E, 1), tri)


def _row_copy(src_ref, src_row, dst_ref, dst_row, sem):
    return pltpu.make_async_copy(src_ref.at[src_row], dst_ref.at[dst_row], sem)


def _dispatch_kernel(tok_ref, h_ref, o_ref, buf, sem, *, tg):
    i = pl.program_id(0)

    def start_block(blk, slot):
        base = blk * tg

        def issue(r, c):
            _row_copy(h_ref, tok_ref[base + r], buf.at[slot], r, sem.at[slot]).start()
            return c

        lax.fori_loop(0, tg, issue, 0, unroll=8)

    @pl.when(i == 0)
    def _():
        start_block(0, 0)

    @pl.when(i + 1 < pl.num_programs(0))
    def _():
        start_block(i + 1, (i + 1) & 1)

    slot = i & 1
    pltpu.make_async_copy(h_ref.at[pl.ds(0, tg)], buf.at[slot], sem.at[slot]).wait()
    o_ref[...] = buf[slot].reshape(tg, D)


def _dispatch_call(slot_token, h, *, tg):
    n_slots = slot_token.shape[0]
    return pl.pallas_call(
        functools.partial(_dispatch_kernel, tg=tg),
        grid_spec=pltpu.PrefetchScalarGridSpec(
            num_scalar_prefetch=1,
            grid=(n_slots // tg,),
            in_specs=[pl.BlockSpec(memory_space=pl.ANY)],
            out_specs=pl.BlockSpec((tg, D), lambda i, tok: (i, 0)),
            scratch_shapes=[pltpu.VMEM((2, tg, D // LANES, LANES), F32), pltpu.SemaphoreType.DMA((2,))],
        ),
        out_shape=jax.ShapeDtypeStruct((n_slots, D), F32),
        compiler_params=_cparams(("arbitrary",)),
        name="moe_dispatch",
    )(slot_token, h)


def _gateup_kernel(ie_ref, in_ref, im_ref, iv_ref, x_ref, wg_ref, wu_ref, o_ref):
    @pl.when(iv_ref[pl.program_id(0)] == 1)
    def _():
        x = x_ref[...]
        g = _dot(x, wg_ref[0, 0])
        u = _dot(x, wu_ref[0, 0])
        o_ref[...] = ((g * _sigmoid(g)) * u).astype(BF16)

    @pl.when(iv_ref[pl.program_id(0)] == 0)
    def _():
        o_ref[...] = jnp.zeros_like(o_ref)


def _down_kernel(ie_ref, in_ref, im_ref, iv_ref, h_ref, wd_ref, o_ref):
    @pl.when(iv_ref[pl.program_id(0)] == 1)
    def _():
        y = lax.dot_general(h_ref[...], wd_ref[0, 0], (((1,), (0,)), ((), ())), preferred_element_type=F32)
        o_ref[...] = y.reshape(o_ref.shape)

    @pl.when(iv_ref[pl.program_id(0)] == 0)
    def _():
        o_ref[...] = jnp.zeros_like(o_ref)


def _gateup_call(items, x_sorted, wg, wu, layer, *, tm, tn):
    n_slots = x_sorted.shape[0]
    n_items = items[0].shape[0]
    w_spec = pl.BlockSpec((1, 1, D, tn), lambda i, ie, inn, im, iv: (layer, ie[i], 0, inn[i]))
    return pl.pallas_call(
        _gateup_kernel,
        grid_spec=pltpu.PrefetchScalarGridSpec(
            num_scalar_prefetch=4,
            grid=(n_items,),
            in_specs=[pl.BlockSpec((tm, D), lambda i, ie, inn, im, iv: (im[i], 0)), w_spec, w_spec],
            out_specs=pl.BlockSpec((tm, tn), lambda i, ie, inn, im, iv: (im[i], inn[i])),
        ),
        out_shape=jax.ShapeDtypeStruct((n_slots, FF), BF16),
        compiler_params=_cparams(("arbitrary",)),
        name="moe_gateup",
    )(*items, x_sorted, wg, wu)


def _down_call(items, hs, wd, layer, *, tm, tn):
    n_slots = hs.shape[0]
    n_items = items[0].shape[0]
    return pl.pallas_call(
        _down_kernel,
        grid_spec=pltpu.PrefetchScalarGridSpec(
            num_scalar_prefetch=4,
            grid=(n_items,),
            in_specs=[pl.BlockSpec((tm, FF), lambda i, ie, inn, im, iv: (im[i], 0)),
                      pl.BlockSpec((1, 1, FF, tn), lambda i, ie, inn, im, iv: (layer, ie[i], 0, inn[i]))],
            out_specs=pl.BlockSpec((tm, tn // LANES, LANES), lambda i, ie, inn, im, iv: (im[i], inn[i], 0)),
        ),
        out_shape=jax.ShapeDtypeStruct((n_slots, D // LANES, LANES), F32),
        compiler_params=_cparams(("arbitrary",)),
        name="moe_down",
    )(*items, hs, wd)


def _moe_items(blocks_per_expert, n_tiles, n_blocks):
    bends = jnp.cumsum(blocks_per_expert)
    bstart = bends - blocks_per_expert
    total = bends[-1] * n_tiles
    step = jnp.arange(n_blocks * n_tiles, dtype=I32)
    j = jnp.minimum(step, total - 1)
    e = jnp.minimum(jnp.sum((j[:, None] >= (bends * n_tiles)[None, :]).astype(I32), axis=1), N_E - 1)
    local = j - bstart[e] * n_tiles
    nb = jnp.maximum(blocks_per_expert[e], 1)
    valid = step < total
    tail = jnp.maximum(step - total, 0)
    col = jnp.where(valid, local // nb, tail % n_tiles)
    row = jnp.where(valid, bstart[e] + local % nb, bends[-1] + tail // n_tiles)
    return e, col.astype(I32), row.astype(I32), valid.astype(I32)


def _combine_kernel(dest_ref, y_ref, x_ref, gm_ref, gw_ref, o_ref, buf, sem, *, tb, t_off, t_total):
    i = pl.program_id(0)

    def start_block(blk, slot):
        base = t_off + blk * tb

        def issue(r, c):
            for k in range(2):
                _row_copy(y_ref, dest_ref[k * t_total + base + r], buf.at[slot, k], r, sem.at[slot]).start()
            return c

        lax.fori_loop(0, tb, issue, 0, unroll=4)

    @pl.when(i == 0)
    def _():
        start_block(0, 0)

    @pl.when(i + 1 < pl.num_programs(0))
    def _():
        start_block(i + 1, (i + 1) & 1)

    slot = i & 1
    for k in range(2):
        pltpu.make_async_copy(y_ref.at[pl.ds(0, tb)], buf.at[slot, k], sem.at[slot]).wait()
    gw = gw_ref[...]
    y0 = buf[slot, 0].reshape(tb, D)
    y1 = buf[slot, 1].reshape(tb, D)
    o_ref[...] = x_ref[...] + gm_ref[...] * (gw[:, 0:1] * y0 + gw[:, 1:2] * y1)


def _combine_call(dest_flat, y, x, gm, gw, *, tb, t_off, t_total, name):
    m = x.shape[0]
    if gm.shape[0] == 1:
        gm_spec = pl.BlockSpec((1, D), lambda i, d: (0, 0))
    else:
        gm_spec = pl.BlockSpec((tb, D), lambda i, d: (i, 0))
    return pl.pallas_call(
        functools.partial(_combine_kernel, tb=tb, t_off=t_off, t_total=t_total),
        grid_spec=pltpu.PrefetchScalarGridSpec(
            num_scalar_prefetch=1,
            grid=(m // tb,),
            in_specs=[pl.BlockSpec(memory_space=pl.ANY),
                      pl.BlockSpec((tb, D), lambda i, d: (i, 0)), gm_spec,
                      pl.BlockSpec((tb, 2), lambda i, d: (i, 0))],
            out_specs=pl.BlockSpec((tb, D), lambda i, d: (i, 0)),
            scratch_shapes=[pltpu.VMEM((2, 2, tb, D // LANES, LANES), F32), pltpu.SemaphoreType.DMA((2,))],
        ),
        out_shape=jax.ShapeDtypeStruct((m, D), F32),
        compiler_params=_cparams(("arbitrary",)),
        name=name,
    )(dest_flat, y, x, gm, gw)


def _moe(layer, h_all, logits_all, w_router_bias, wg, wu, wd, xp, xs, gmp, gms):
    t = h_all.shape[0]
    tm = MOE_TM
    eid, gw, rank, cnt = _route_call(logits_all[:, :N_E].T, w_router_bias, tb=640)
    counts = cnt[:, 0]
    blocks_per_expert = (counts + tm - 1) // tm
    pstart = (jnp.cumsum(blocks_per_expert) - blocks_per_expert) * tm
    first = jnp.sum(jnp.where(eid[None] == jnp.arange(N_E, dtype=I32)[:, None, None],
                              pstart[:, None, None], 0), axis=0)
    dest = first + rank
    n_blocks = (2 * t) // tm + N_E
    n_slots = n_blocks * tm
    tok = jnp.tile(jnp.arange(t, dtype=I32), 2)
    slot_token = jnp.zeros((n_slots,), I32).at[dest.reshape(-1)].set(tok)
    x_sorted = _dispatch_call(slot_token, h_all, tg=MOE_DISPATCH_ROWS)
    items1 = _moe_items(blocks_per_expert, FF // MOE_FF_TN, n_blocks)
    hs = _gateup_call(items1, x_sorted, wg, wu, layer, tm=tm, tn=MOE_FF_TN)
    items2 = _moe_items(blocks_per_expert, D // MOE_D_TN, n_blocks)
    y = _down_call(items2, hs, wd, layer, tm=tm, tn=MOE_D_TN)
    dest_flat = dest.reshape(-1)
    gwt = gw.T
    np_ = xp.shape[0]
    xp2 = _combine_call(dest_flat, y, xp, gmp, gwt[:np_], tb=256, t_off=0, t_total=t, name="moe_combine_p")
    xs2 = _combine_call(dest_flat, y, xs, gms, gwt[np_:], tb=128, t_off=np_, t_total=t, name="moe_combine_s")
    return xp2, xs2


def _attn_kernel(lam_ref, qt_ref, k_ref, vt_ref, g_ref, o_ref, m_scr, l_scr, acc_scr, sa_scr, sb_scr,
                 *, tq, out_scale):
    qi = pl.program_id(1)
    qt = qt_ref[0, 0]
    feat = lax.broadcasted_iota(I32, (C_VD, tq), 0)
    zero = jnp.zeros_like(qt)
    qq = jnp.concatenate([jnp.where(feat < C_HD, qt, zero), jnp.where(feat >= C_HD, qt, zero)], axis=1)
    m_scr[...] = jnp.full_like(m_scr, -jnp.inf)
    l_scr[...] = jnp.zeros_like(l_scr)
    acc_scr[...] = jnp.zeros_like(acc_scr)

    def scores(j, dst, masked):
        off = pl.multiple_of(j * tq, tq)
        st = _dot(k_ref[pl.ds(off, tq), :], qq)
        if masked:
            key = lax.broadcasted_iota(I32, (tq, 2 * tq), 0)
            col = lax.broadcasted_iota(I32, (tq, 2 * tq), 1)
            st = jnp.where(key <= jnp.where(col >= tq, col - tq, col), st, -jnp.inf)
        dst[...] = st

    def absorb(src, j):
        st = src[...]
        m_prev = m_scr[...]
        m_new = jnp.maximum(m_prev, jnp.max(st, axis=0, keepdims=True))
        alpha = jnp.exp2(m_prev - m_new)
        p = jnp.exp2(st - m_new[0:1, :])
        l_scr[...] = alpha * l_scr[...] + jnp.sum(p, axis=0, keepdims=True)
        acc_scr[...] = alpha[0:1, :] * acc_scr[...] + _dot(vt_ref[0, j], p.astype(BF16))
        m_scr[...] = m_new

    scores(qi, sa_scr, True)
    npair = lax.shift_right_logical(qi, 1)

    def pair(t, c):
        scores(2 * t, sb_scr, False)
        absorb(sa_scr, jnp.where(t == 0, qi, 2 * t - 1))
        scores(2 * t + 1, sa_scr, False)
        absorb(sb_scr, 2 * t)
        return c

    lax.fori_loop(0, npair, pair, 0)
    held = jnp.where(npair == 0, qi, 2 * npair - 1)

    @pl.when((qi & 1) == 1)
    def _():
        scores(qi - 1, sb_scr, False)
        absorb(sa_scr, held)
        absorb(sb_scr, qi - 1)

    @pl.when((qi & 1) == 0)
    def _():
        absorb(sa_scr, held)

    on = acc_scr[...] * (1.0 / l_scr[0:1, :])
    o = (on[:, 0:tq] - lam_ref[0] * on[:, tq:2 * tq]).T
    o = o * lax.rsqrt(jnp.mean(o * o, axis=-1, keepdims=True) + EPS) * g_ref[...] * out_scale
    o_ref[...] = o.astype(o_ref.dtype)


def _attn_call(lam, qt, k, vt, g, *, tq, out_scale):
    s = k.shape[0]
    return pl.pallas_call(
        functools.partial(_attn_kernel, tq=tq, out_scale=out_scale),
        grid=(C_H, s // tq),
        in_specs=[
            pl.BlockSpec(memory_space=pltpu.SMEM),
            pl.BlockSpec((1, 1, C_VD, tq), lambda h, i: (h, i, 0, 0)),
            pl.BlockSpec((s, C_VD), lambda h, i: (0, h)),
            pl.BlockSpec((1, s // tq, C_VD, tq), lambda h, i: (h, 0, 0, 0)),
            pl.BlockSpec((1, C_VD), lambda h, i: (0, 0)),
        ],
        out_specs=pl.BlockSpec((tq, C_VD), lambda h, i: (i, h)),
        out_shape=jax.ShapeDtypeStruct((s, C_H * C_VD), BF16),
        scratch_shapes=[pltpu.VMEM((SUBLANES, 2 * tq), F32), pltpu.VMEM((SUBLANES, 2 * tq), F32),
                        pltpu.VMEM((C_VD, 2 * tq), F32),
                        pltpu.VMEM((tq, 2 * tq), F32), pltpu.VMEM((tq, 2 * tq), F32)],
        compiler_params=_cparams(("parallel", "arbitrary")),
        name="diff_attn_prompt",
    )(lam, qt, k, vt, g)


def _sattn_kernel(pt_ref, lam_ref, wq_ref, bias_ref, *rest, out_scale, n_pg):
    k_refs, v_refs = rest[:n_pg], rest[n_pg:2 * n_pg]
    kn_ref, vn_ref, g_ref, o_ref, m_scr, l_scr, acc_scr = rest[2 * n_pg:]
    b = pl.program_id(0)
    p = pl.program_id(1)

    @pl.when(p == 0)
    def _():
        m_scr[...] = jnp.full_like(m_scr, -jnp.inf)
        l_scr[...] = jnp.zeros_like(l_scr)
        acc_scr[...] = jnp.zeros_like(acc_scr)

    wq = wq_ref[0]
    reps = kn_ref.shape[0] // LANES

    def update(s, vmat):
        m_prev = m_scr[...]
        m_new = jnp.maximum(m_prev, jnp.max(s, axis=1, keepdims=True))
        alpha = jnp.exp2(m_prev - m_new)
        pr = jnp.exp2(s - jnp.concatenate([m_new] * reps, axis=1))
        l_scr[...] = alpha * l_scr[...] + jnp.sum(pr, axis=1, keepdims=True)
        acc_scr[...] = alpha * acc_scr[...] + _dot(pr, vmat)
        m_scr[...] = m_new

    for i in range(n_pg):
        update(_dot_nt(wq, k_refs[i][0]) + bias_ref[...], v_refs[i][0])

    @pl.when(p == pl.num_programs(1) - 1)
    def _():
        n = kn_ref.shape[0]
        rows = 2 * DEC_S
        s = _dot_nt(wq, kn_ref[...])
        c = lax.broadcasted_iota(I32, (wq.shape[0], n), 0)
        col = lax.broadcasted_iota(I32, (wq.shape[0], n), 1)
        tok = lax.shift_right_logical(col, 4)
        ok = ((col & (C_H - 1)) == lax.shift_right_logical(c, 3)) \
            & (lax.shift_right_logical(tok, 2) == b) & ((tok & (DEC_S - 1)) <= (c & (DEC_S - 1)))
        update(jnp.where(ok, s, -jnp.inf), vn_ref[...])
        out = acc_scr[...] / l_scr[...]
        for h in range(C_H):
            blk = out[h * rows:(h + 1) * rows, :]
            o = blk[0:DEC_S, :] - lam_ref[0] * blk[DEC_S:rows, :]
            on = o * lax.rsqrt(jnp.mean(o * o, axis=-1, keepdims=True) + EPS) * g_ref[...] * out_scale
            o_ref[0, :, h * C_VD:(h + 1) * C_VD] = on


def _sattn_call(page_table, lam, wq, bias, ck, cv, kn, vn, g, *, out_scale, n_pg):
    nb, n_pages = page_table.shape
    nrow = wq.shape[1]
    prow = ck.shape[1]

    def page_spec(i):
        return pl.BlockSpec((1, prow, C_VD), lambda b, p, pt: (pt[b * n_pages + p * n_pg + i], 0, 0))

    const2 = lambda b, p, pt: (0, 0)
    return pl.pallas_call(
        functools.partial(_sattn_kernel, out_scale=out_scale, n_pg=n_pg),
        grid_spec=pltpu.PrefetchScalarGridSpec(
            num_scalar_prefetch=1,
            grid=(nb, n_pages // n_pg),
            in_specs=[
                pl.BlockSpec(memory_space=pltpu.SMEM),
                pl.BlockSpec((1, nrow, C_VD), lambda b, p, pt: (b, 0, 0)),
                pl.BlockSpec(bias.shape, const2),
                *[page_spec(i) for i in range(n_pg)],
                *[page_spec(i) for i in range(n_pg)],
                pl.BlockSpec(kn.shape, const2),
                pl.BlockSpec(vn.shape, const2),
                pl.BlockSpec((1, C_VD), const2),
            ],
            out_specs=pl.BlockSpec((1, DEC_S, C_H * C_VD), lambda b, p, pt: (b, 0, 0)),
            scratch_shapes=[pltpu.VMEM((nrow, LANES), F32), pltpu.VMEM((nrow, LANES), F32),
                            pltpu.VMEM((nrow, C_VD), F32)],
        ),
        out_shape=jax.ShapeDtypeStruct((nb, DEC_S, C_H * C_VD), F32),
        compiler_params=_cparams(("parallel", "arbitrary")),
        name="diff_attn_sample",
    )(page_table.reshape(-1), lam, wq, bias, *([ck] * n_pg), *([cv] * n_pg), kn, vn, g)


def _rope0_tables(pos):
    half = B_DK // 2
    freqs = 10000.0 ** (-jnp.arange(half, dtype=F32) / half)
    ang = pos.astype(F32)[:, None] * freqs[None, :]
    return jnp.cos(ang), jnp.sin(ang)


def _rope1_tables(pos):
    half = C_ROT // 2
    freqs = 500000.0 ** (-jnp.arange(half, dtype=F32) / half)
    ang = pos.astype(F32)[:, None] * freqs[None, :]
    cos, sin = jnp.cos(ang), jnp.sin(ang)
    n = pos.shape[0]
    pad = C_HD - C_ROT
    cos_t = jnp.concatenate([cos, cos, jnp.ones((n, pad), F32)], axis=1)
    sin_a = jnp.concatenate([-sin, jnp.zeros((n, half + pad), F32)], axis=1)
    sin_b = jnp.concatenate([jnp.zeros((n, half), F32), sin, jnp.zeros((n, pad), F32)], axis=1)
    rep = lambda t: jnp.tile(t, (1, LANES // C_HD))
    return rep(cos_t), rep(sin_a), rep(sin_b)


def _log_gamma():
    return jnp.log(1.0 - 2.0 ** (-5.0 - jnp.arange(B_H, dtype=F32)))


def _lanes_per_head(per_head):
    return jnp.repeat(per_head, B_DK, axis=1)


def kernel(x_prompt, x_sample, c_prompt, c_sample, state_ret, cache_k, cache_v, page_table, w_ada, b_ada, norm_mix, norm_ffn, w_in0, gmlp_ln_g, gmlp_ln_b, gmlp_w_s, gmlp_b_s, ret_norm_g, w_out0, w_in1, q_norm_g, k_norm_g, lambda_q1, lambda_k1, lambda_q2, lambda_k2, subln_g, w_out1, w_router, router_bias, w_gate, w_up, w_down):
    xp = x_prompt.reshape(SEQ, D)
    xs = x_sample.reshape(NS, D)
    pos_p = jnp.arange(SEQ)
    pos_s = PAST + jnp.arange(DEC_S)

    mc = 1 + DEC_B
    mc_pad = -(-mc // SUBLANES) * SUBLANES
    c_all = jnp.concatenate([c_prompt, c_sample, jnp.zeros((mc_pad - mc, D), F32)], axis=0)
    mods = _ada_call(c_all, w_ada, b_ada).reshape(2, mc_pad, 6, D)
    wr_pad = jnp.pad(w_router, ((0, 0), (0, LANES - N_E)))

    def mod(l, i):
        return mods[l, 0:1, i], jnp.repeat(mods[l, 1:mc, i], DEC_S, axis=0)

    l = 0
    (shp, shs), (scp, scs), (gtp, gts) = mod(l, 0), mod(l, 1), mod(l, 2)
    (sh2p, sh2s), (sc2p, sc2s), (gt2p, gt2s) = mod(l, 3), mod(l, 4), mod(l, 5)
    gmix = norm_mix[l].reshape(1, D)
    w_in = w_in0[0].astype(BF16)
    pp = _inproj_call(xp, gmix, scp, shp, w_in, mode="f32", tm=512, tn=1024, name="inproj0_p")[0]
    ps = _inproj_call(xs, gmix, scs, shs, w_in, mode="f32", tm=NS, tn=1024, name="inproj0_s")[0]

    log_g = _log_gamma()
    lng, lnb = gmlp_ln_g[0].reshape(1, A_W), gmlp_ln_b[0].reshape(1, A_W)
    rg = ret_norm_g[0].reshape(1, B_H * B_DK)
    ch = 128
    idx = jnp.arange(ch, dtype=F32)
    rel = idx[:, None] - idx[None, :]
    dec = jnp.where(rel[None] >= 0, jnp.exp(jnp.maximum(rel, 0.0)[None] * log_g[:, None, None]), 0.0)
    cd = _lanes_per_head(jnp.exp((idx + 1.0)[:, None] * log_g[None, :]))
    kd = _lanes_per_head(jnp.exp((ch - 1.0 - idx)[:, None] * log_g[None, :]))
    sdec = jnp.exp(ch * log_g)
    wm = jnp.where(jnp.tril(jnp.ones((ch, ch), dtype=bool))[None], gmlp_w_s[0], 0.0).astype(BF16)
    bs = jnp.repeat(gmlp_b_s[0].T, A_W // A_H, axis=1)
    cos0p, sin0p = _rope0_tables(pos_p)
    mix_p, ret_p = _mix0p_call(pp, lng, lnb, wm, bs, cos0p, sin0p, dec, cd, kd, rg, sdec)

    rows = 2 * DEC_S
    tloc = jnp.arange(rows) % DEC_S
    ws4 = gmlp_w_s[0][:, :DEC_S, :DEC_S]
    cw = []
    dm = []
    for d in range(DEC_S):
        src = tloc - d
        ok = src >= 0
        coef = jnp.where(ok[:, None], ws4[:, tloc, jnp.maximum(src, 0)].T, 0.0)
        cw.append(jnp.repeat(coef, A_W // A_H, axis=1))
        dm.append(_lanes_per_head(jnp.where(ok[:, None], jnp.exp(float(d) * log_g)[None, :], 0.0)))
    cw, dm = jnp.stack(cw), jnp.stack(dm)
    bs_s = jnp.repeat(gmlp_b_s[0][:, :DEC_S].T[tloc], A_W // A_H, axis=1)
    tl = tloc.astype(F32)
    cd_s = _lanes_per_head(jnp.exp((tl + 1.0)[:, None] * log_g[None, :]))
    kd_s = _lanes_per_head(jnp.exp((DEC_S - 1.0 - tl)[:, None] * log_g[None, :]))
    sdec_s = jnp.exp(DEC_S * log_g)
    cos0s, sin0s = _rope0_tables(pos_s)
    mix_s, gv_s, ret_s = _mix0s_call(ps, state_ret[0], lng, lnb, cw, bs_s, cos0s[tloc], sin0s[tloc],
                                     dm, cd_s, kd_s, rg, sdec_s)

    gffn = norm_ffn[l].reshape(1, D)
    w_out = w_out0[0].astype(BF16)
    xp, h2p, lgp = _outproj_call(mix_p, w_out, xp, gtp, gffn, sc2p, sh2p, wr_pad, tm=256, name="outproj0_p")
    xs, h2s, lgs = _outproj_call(mix_s, w_out, xs, gts, gffn, sc2s, sh2s, wr_pad, tm=NS, name="outproj0_s")
    xp, xs = _moe(l, jnp.concatenate([h2p, h2s], axis=0), jnp.concatenate([lgp, lgs], axis=0), router_bias,
                  w_gate, w_up, w_down, xp, xs, gt2p, gt2s)

    l = 1
    (shp, shs), (scp, scs), (gtp, gts) = mod(l, 0), mod(l, 1), mod(l, 2)
    (sh2p, sh2s), (sc2p, sc2s), (gt2p, gt2s) = mod(l, 3), mod(l, 4), mod(l, 5)
    gmix = norm_mix[l].reshape(1, D)
    lambda_init = 0.8 - 0.6 * math.exp(-0.3 * l)
    lam = (jnp.exp(jnp.sum(lambda_q1[0] * lambda_k1[0])) - jnp.exp(jnp.sum(lambda_q2[0] * lambda_k2[0]))
           + lambda_init).reshape(1).astype(F32)
    hw = C_H * C_VD
    wq_w = w_in1[0][:, 0:hw].astype(BF16)
    wk_w = w_in1[0][:, hw:2 * hw].astype(BF16)
    wv_w = w_in1[0][:, 2 * hw:3 * hw].astype(BF16)
    gq = jnp.tile(q_norm_g[0], LANES // C_HD).reshape(1, LANES)
    gk = jnp.tile(k_norm_g[0], LANES // C_HD).reshape(1, LANES)
    lane = jnp.arange(LANES)
    bd = (lane[:, None] // C_HD == lane[None, :] // C_HD).astype(BF16)
    tabs_p = _rope1_tables(pos_p)
    tabs_s = tuple(jnp.tile(t, (DEC_B, 1)) for t in _rope1_tables(pos_s))
    qscale = C_HD ** -0.5 * math.log2(math.e)

    q_p = _inproj_call(xp, gmix, scp, shp, wq_w, mode="q", tm=512, tn=1024, extras=(gq, *tabs_p, bd),
                       scale=qscale, name="inproj1_q_p")[0]
    k_p, k_pb = _inproj_call(xp, gmix, scp, shp, wk_w, mode="k", tm=512, tn=1024, extras=(gk, *tabs_p, bd),
                             name="inproj1_k_p")
    v_p, v_pb = _inproj_call(xp, gmix, scp, shp, wv_w, mode="dual", tm=512, tn=1024, name="inproj1_v_p")
    q_s = _inproj_call(xs, gmix, scs, shs, wq_w, mode="q", tm=NS, tn=1024, extras=(gq, *tabs_s, bd),
                       scale=qscale, name="inproj1_q_s")[0]
    k_s, k_sb = _inproj_call(xs, gmix, scs, shs, wk_w, mode="k", tm=NS, tn=1024, extras=(gk, *tabs_s, bd),
                             name="inproj1_k_s")
    v_s, v_sb = _inproj_call(xs, gmix, scs, shs, wv_w, mode="dual", tm=NS, tn=1024, name="inproj1_v_s")

    gsub = subln_g[0].reshape(1, C_VD)
    out_scale = 1.0 - lambda_init
    nqb = SEQ // ATT_TQ
    to_tiles = lambda t: t.reshape(nqb, ATT_TQ, C_H, C_VD).transpose(2, 0, 3, 1)
    att_p = _attn_call(lam, to_tiles(q_p), k_pb, to_tiles(v_pb), gsub, tq=ATT_TQ, out_scale=out_scale)

    q5 = q_s.astype(F32).reshape(DEC_B, DEC_S, C_H, 2, C_HD).transpose(0, 2, 3, 1, 4)
    wq_rows = (q5[:, :, :, :, None, :] * jnp.eye(2, dtype=F32)[None, None, :, None, :, None]
               ).reshape(DEC_B, C_H * 2 * DEC_S, C_VD)
    n_pool = cache_k.shape[1]
    prow = PAGE * C_H
    srow = jnp.arange(C_H * 2 * DEC_S)[:, None] // (2 * DEC_S)
    head_bias = jnp.where(jnp.arange(prow)[None, :] % C_H == srow, 0.0, -jnp.inf).astype(F32)
    att_s = _sattn_call(page_table, lam, wq_rows, head_bias,
                        cache_k.reshape(n_pool, prow, C_VD), cache_v.reshape(n_pool, prow, C_VD),
                        k_s.reshape(NS * C_H, C_VD), v_s.reshape(NS * C_H, C_VD), gsub,
                        out_scale=out_scale, n_pg=SATT_PAGES)
    att_s = att_s.reshape(NS, hw)

    gffn = norm_ffn[l].reshape(1, D)
    w_out = w_out1[0].astype(BF16)
    xp, h2p, lgp = _outproj_call(att_p, w_out, xp, gtp, gffn, sc2p, sh2p, wr_pad, tm=256, name="outproj1_p")
    xs, h2s, lgs = _outproj_call(att_s, w_out, xs, gts, gffn, sc2s, sh2s, wr_pad, tm=NS, name="outproj1_s")
    xp, xs = _moe(l, jnp.concatenate([h2p, h2s], axis=0), jnp.concatenate([lgp, lgs], axis=0), router_bias,
                  w_gate, w_up, w_down, xp, xs, gt2p, gt2s)

    return (xp.reshape(1, SEQ, D), xs.reshape(DEC_B, DEC_S, D),
            ret_p.reshape(1, 1, B_H, B_DK, B_DK), ret_s.reshape(1, DEC_B, B_H, B_DK, B_DK),
            gv_s.reshape(1, DEC_B, DEC_S, A_W),
            k_p.reshape(1, 1, SEQ, C_H, C_VD), v_p.reshape(1, 1, SEQ, C_H, C_VD),
            k_s.reshape(1, DEC_B, DEC_S, C_H, C_VD), v_s.reshape(1, DEC_B, DEC_S, C_H, C_VD))
```

```python
import functools
import math

import jax
import jax.numpy as jnp
from jax import lax
from jax.experimental import pallas as pl
from jax.experimental.pallas import tpu as pltpu

F32 = jnp.float32
BF16 = jnp.bfloat16
I32 = jnp.int32
HIGHEST = lax.Precision.HIGHEST

D = 2048
SEQ = 8192
DEC_B = 32
DEC_S = 4
NS = DEC_B * DEC_S
PAST = 8192
PAGE = 128
EPS = 1e-6
A_W = 1024
A_H = 8
B_H = 4
B_DK = 256
C_H = 16
C_HD = 64
C_VD = 128
C_ROT = 16
N_E = 16
N_G = 4
FF = 1536

LANES = 128
SUBLANES = 8
VMEM_LIMIT_BYTES = 56 * 1024 * 1024

MOE_TM = 256
MOE_FF_TN = 768
MOE_D_TN = 1024
MOE_DISPATCH_ROWS = 3 * MOE_TM
ROW_DMA_UNROLL = 8
ATT_TQ = 512
SATT_PAGES = 4


def _cparams(sem):
    return pltpu.CompilerParams(dimension_semantics=sem, vmem_limit_bytes=VMEM_LIMIT_BYTES)


def _sigmoid(x):
    return 1.0 / (1.0 + jnp.exp(-x))


def _gelu(x):
    return x * (0.5 * (1.0 + jnp.tanh(0.7978845608028654 * (x + 0.044715 * (x * x * x)))))


def _rms_mod(x, g, sc, sh):
    ms = jnp.mean(x * x, axis=-1, keepdims=True)
    return (x * lax.rsqrt(ms + EPS)) * g * (1.0 + sc) + sh


def _dot(a, b):
    return jnp.dot(a, b, preferred_element_type=F32)


def _dot_nt(a, b):
    return lax.dot_general(a, b, (((1,), (1,)), ((), ())), preferred_element_type=F32)


def _dot_tn(a, b):
    return lax.dot_general(a, b, (((0,), (0,)), ((), ())), preferred_element_type=F32)


def _ada_kernel(c_ref, w_ref, b_ref, o_ref):
    c = c_ref[...]
    s = c * _sigmoid(c)
    o_ref[0] = jnp.dot(s, w_ref[0], preferred_element_type=F32, precision=HIGHEST) + b_ref[0]


def _ada_call(c_all, w_ada, b_ada):
    nl, _, n = w_ada.shape
    mc = c_all.shape[0]
    tn = 1024
    return pl.pallas_call(
        _ada_kernel,
        grid=(nl, n // tn),
        in_specs=[
            pl.BlockSpec((mc, D), lambda l, j: (0, 0)),
            pl.BlockSpec((1, D, tn), lambda l, j: (l, 0, j)),
            pl.BlockSpec((1, 1, tn), lambda l, j: (l, 0, j)),
        ],
        out_specs=pl.BlockSpec((1, mc, tn), lambda l, j: (l, 0, j)),
        out_shape=jax.ShapeDtypeStruct((nl, mc, n), F32),
        compiler_params=_cparams(("parallel", "parallel")),
        name="ada",
    )(c_all, w_ada, b_ada.reshape(nl, 1, n))


def _inproj_kernel(*refs, mode, scale, tn):
    x_ref, g_ref, sc_ref, sh_ref, w_ref = refs[:5]
    h_scr = refs[-1]

    @pl.when(pl.program_id(1) == 0)
    def _():
        h_scr[...] = _rms_mod(x_ref[...], g_ref[...], sc_ref[...], sh_ref[...]).astype(BF16)

    acc = _dot(h_scr[...], w_ref[...])
    if mode in ("q", "k"):
        gq_ref, cos_ref, sa_ref, sb_ref, bd_ref = refs[5:10]
        outs = refs[10:-1]
        gq = gq_ref[...]
        cos_t, sin_a, sin_b, bd = cos_ref[...], sa_ref[...], sb_ref[...], bd_ref[...]
        for j in range(tn // LANES):
            sl = slice(j * LANES, (j + 1) * LANES)
            y = acc[:, sl]
            sq = y * y
            hi = sq.astype(BF16)
            lo = (sq - hi.astype(F32)).astype(BF16)
            ss = _dot(hi, bd) + _dot(lo, bd)
            yn = y * lax.rsqrt(ss * (1.0 / C_HD) + EPS) * gq
            r = yn * cos_t + pltpu.roll(yn, LANES - C_ROT // 2, 1) * sin_a + pltpu.roll(yn, C_ROT // 2, 1) * sin_b
            if scale != 1.0:
                r = r * scale
            if mode == "q":
                outs[0][:, sl] = r.astype(BF16)
            else:
                outs[0][:, sl] = r
                outs[1][:, sl] = r.astype(BF16)
    elif mode == "dual":
        outs = refs[5:-1]
        outs[0][...] = acc
        outs[1][...] = acc.astype(BF16)
    else:
        refs[5][...] = acc


def _inproj_call(x, g, sc, sh, w, *, mode, tm, tn, extras=(), scale=1.0, name="inproj"):
    m = x.shape[0]
    n = w.shape[1]

    def rows(a):
        if a.shape[0] == 1:
            return pl.BlockSpec((1, D), lambda i, j: (0, 0))
        return pl.BlockSpec((tm, D), lambda i, j: (i, 0))

    in_specs = [
        pl.BlockSpec((tm, D), lambda i, j: (i, 0)),
        pl.BlockSpec((1, D), lambda i, j: (0, 0)),
        rows(sc),
        rows(sh),
        pl.BlockSpec((D, tn), lambda i, j: (0, j)),
    ]
    args = [x, g, sc, sh, w]
    o_spec = pl.BlockSpec((tm, tn), lambda i, j: (i, j))
    if mode in ("q", "k"):
        gq, cos_t, sin_a, sin_b, bd = extras
        in_specs += [
            pl.BlockSpec((1, LANES), lambda i, j: (0, 0)),
            pl.BlockSpec((tm, LANES), lambda i, j: (i, 0)),
            pl.BlockSpec((tm, LANES), lambda i, j: (i, 0)),
            pl.BlockSpec((tm, LANES), lambda i, j: (i, 0)),
            pl.BlockSpec((LANES, LANES), lambda i, j: (0, 0)),
        ]
        args += [gq, cos_t, sin_a, sin_b, bd]
    if mode == "q":
        out_shape = [jax.ShapeDtypeStruct((m, n), BF16)]
    elif mode in ("k", "dual"):
        out_shape = [jax.ShapeDtypeStruct((m, n), F32), jax.ShapeDtypeStruct((m, n), BF16)]
    else:
        out_shape = [jax.ShapeDtypeStruct((m, n), F32)]
    res = pl.pallas_call(
        functools.partial(_inproj_kernel, mode=mode, scale=scale, tn=tn),
        grid=(m // tm, n // tn),
        in_specs=in_specs,
        out_specs=[o_spec] * len(out_shape),
        out_shape=out_shape,
        scratch_shapes=[pltpu.VMEM((tm, D), BF16)],
        compiler_params=_cparams(("parallel", "arbitrary")),
        name=name,
    )(*args)
    return res


def _mix0p_kernel(sdec_ref, p_ref, lng_ref, lnb_ref, wm_ref, bs_ref, cos_ref, sin_ref, dec_ref,
                  cd_ref, kd_ref, rg_ref, mix_ref, so_ref, st_scr):
    c = pl.program_id(0)

    @pl.when(c == 0)
    def _():
        st_scr[...] = jnp.zeros_like(st_scr)

    u = _gelu(p_ref[:, 0:A_W])
    v = _gelu(p_ref[:, A_W:2 * A_W])
    mu = jnp.mean(v, axis=-1, keepdims=True)
    vc = v - mu
    v = vc * lax.rsqrt(jnp.mean(vc * vc, axis=-1, keepdims=True) + EPS) * lng_ref[...] + lnb_ref[...]
    vb = v.astype(BF16)
    for hh in range(A_H):
        sl = slice(hh * LANES, (hh + 1) * LANES)
        sv = _dot(wm_ref[hh], vb[:, sl]) + bs_ref[:, sl]
        mix_ref[:, sl] = (u[:, sl] * sv).astype(BF16)

    cos, sin = cos_ref[...], sin_ref[...]
    half = B_DK // 2

    def rope(t):
        t1, t2 = t[:, :half], t[:, half:]
        return jnp.concatenate([t1 * cos - t2 * sin, t2 * cos + t1 * sin], axis=1)

    base = 2 * A_W
    hw = B_H * B_DK
    for hd in range(B_H):
        sl = slice(hd * B_DK, (hd + 1) * B_DK)
        q = rope(p_ref[:, base + hd * B_DK: base + (hd + 1) * B_DK])
        k = rope(p_ref[:, base + hw + hd * B_DK: base + hw + (hd + 1) * B_DK]) * (B_DK ** -0.5)
        vr = p_ref[:, base + 2 * hw + hd * B_DK: base + 2 * hw + (hd + 1) * B_DK]
        g = p_ref[:, base + 3 * hw + hd * B_DK: base + 3 * hw + (hd + 1) * B_DK]
        qb, kb, vrb = q.astype(BF16), k.astype(BF16), vr.astype(BF16)
        inner = _dot_nt(qb, kb) * dec_ref[hd]
        st = st_scr[hd]
        o = _dot(inner.astype(BF16), vrb) + _dot(qb, st.astype(BF16)) * cd_ref[:, sl]
        kdb = (k * kd_ref[:, sl]).astype(BF16)
        st_scr[hd] = sdec_ref[hd] * st + _dot_tn(kdb, vrb)
        on = o * lax.rsqrt(jnp.mean(o * o, axis=-1, keepdims=True) + EPS) * rg_ref[:, sl]
        mix_ref[:, A_W + hd * B_DK: A_W + (hd + 1) * B_DK] = ((g * _sigmoid(g)) * on).astype(BF16)

    @pl.when(c == pl.num_programs(0) - 1)
    def _():
        so_ref[...] = st_scr[...]


def _mix0p_call(p, lng, lnb, wm, bs, cos, sin, dec, cd, kd, rg, sdec):
    m = p.shape[0]
    ch = 128
    full = lambda shape: pl.BlockSpec(shape, lambda c: (0,) * len(shape))
    return pl.pallas_call(
        _mix0p_kernel,
        grid=(m // ch,),
        in_specs=[
            pl.BlockSpec(memory_space=pltpu.SMEM),
            pl.BlockSpec((ch, p.shape[1]), lambda c: (c, 0)),
            full((1, A_W)), full((1, A_W)),
            full((A_H, ch, ch)), full((ch, A_W)),
            pl.BlockSpec((ch, LANES), lambda c: (c, 0)),
            pl.BlockSpec((ch, LANES), lambda c: (c, 0)),
            full((B_H, ch, ch)), full((ch, B_H * B_DK)), full((ch, B_H * B_DK)),
            full((1, B_H * B_DK)),
        ],
        out_specs=[
            pl.BlockSpec((ch, D), lambda c: (c, 0)),
            full((B_H, B_DK, B_DK)),
        ],
        out_shape=[jax.ShapeDtypeStruct((m, D), BF16), jax.ShapeDtypeStruct((B_H, B_DK, B_DK), F32)],
        scratch_shapes=[pltpu.VMEM((B_H, B_DK, B_DK), F32)],
        compiler_params=_cparams(("arbitrary",)),
        name="mix0_prompt",
    )(sdec, p, lng, lnb, wm, bs, cos, sin, dec, cd, kd, rg)


def _mix0s_kernel(sdec_ref, p_ref, lng_ref, lnb_ref, cw_ref, bs_ref, cos_ref, sin_ref, dm_ref,
                  cd_ref, kd_ref, rg_ref, st_ref, mix_ref, gv_ref, so_ref):
    rows = 2 * DEC_S

    def shift_rows(t, d):
        return t if d == 0 else pltpu.roll(t, d, 0)

    u = _gelu(p_ref[:, 0:A_W])
    v = _gelu(p_ref[:, A_W:2 * A_W])
    mu = jnp.mean(v, axis=-1, keepdims=True)
    vc = v - mu
    v = vc * lax.rsqrt(jnp.mean(vc * vc, axis=-1, keepdims=True) + EPS) * lng_ref[...] + lnb_ref[...]
    gv_ref[...] = v
    sv = bs_ref[...]
    for d in range(DEC_S):
        sv = sv + cw_ref[d] * shift_rows(v, d)
    mix_ref[:, 0:A_W] = u * sv

    cos, sin = cos_ref[...], sin_ref[...]
    half = B_DK // 2

    def rope(t):
        t1, t2 = t[:, :half], t[:, half:]
        return jnp.concatenate([t1 * cos - t2 * sin, t2 * cos + t1 * sin], axis=1)

    grp = lax.shift_right_logical(lax.broadcasted_iota(I32, (rows, B_DK), 0), 2)
    base = 2 * A_W
    hw = B_H * B_DK
    for hd in range(B_H):
        sl = slice(hd * B_DK, (hd + 1) * B_DK)
        q = rope(p_ref[:, base + hd * B_DK: base + (hd + 1) * B_DK])
        k = rope(p_ref[:, base + hw + hd * B_DK: base + hw + (hd + 1) * B_DK]) * (B_DK ** -0.5)
        vr = p_ref[:, base + 2 * hw + hd * B_DK: base + 2 * hw + (hd + 1) * B_DK]
        g = p_ref[:, base + 3 * hw + hd * B_DK: base + 3 * hw + (hd + 1) * B_DK]
        o = jnp.zeros((rows, B_DK), F32)
        for d in range(DEC_S):
            s = jnp.sum(q * shift_rows(k, d), axis=-1, keepdims=True)
            o = o + (s * dm_ref[d][:, sl]) * shift_rows(vr, d)
        kd = k * kd_ref[:, sl]
        for bi in range(2):
            st = st_ref[bi, hd]
            mine = grp == bi
            o = o + jnp.where(mine, _dot(q, st) * cd_ref[:, sl], 0.0)
            so_ref[bi, hd] = sdec_ref[hd] * st + _dot_tn(jnp.where(mine, kd, 0.0), vr)
        on = o * lax.rsqrt(jnp.mean(o * o, axis=-1, keepdims=True) + EPS) * rg_ref[:, sl]
        mix_ref[:, A_W + hd * B_DK: A_W + (hd + 1) * B_DK] = (g * _sigmoid(g)) * on


def _mix0s_call(p, st_in, lng, lnb, cw, bs, cos, sin, dm, cd, kd, rg, sdec):
    rows = 2 * DEC_S
    nb = st_in.shape[0]
    full = lambda shape: pl.BlockSpec(shape, lambda c: (0,) * len(shape))
    return pl.pallas_call(
        _mix0s_kernel,
        grid=(nb // 2,),
        in_specs=[
            pl.BlockSpec(memory_space=pltpu.SMEM),
            pl.BlockSpec((rows, p.shape[1]), lambda c: (c, 0)),
            full((1, A_W)), full((1, A_W)),
            full((DEC_S, rows, A_W)), full((rows, A_W)),
            full((rows, LANES)), full((rows, LANES)),
            full((DEC_S, rows, B_H * B_DK)), full((rows, B_H * B_DK)), full((rows, B_H * B_DK)),
            full((1, B_H * B_DK)),
            pl.BlockSpec((2, B_H, B_DK, B_DK), lambda c: (c, 0, 0, 0)),
        ],
        out_specs=[
            pl.BlockSpec((rows, D), lambda c: (c, 0)),
            pl.BlockSpec((rows, A_W), lambda c: (c, 0)),
            pl.BlockSpec((2, B_H, B_DK, B_DK), lambda c: (c, 0, 0, 0)),
        ],
        out_shape=[
            jax.ShapeDtypeStruct((nb * DEC_S, D), F32),
            jax.ShapeDtypeStruct((nb * DEC_S, A_W), F32),
            jax.ShapeDtypeStruct(st_in.shape, F32),
        ],
        compiler_params=_cparams(("parallel",)),
        name="mix0_sample",
    )(sdec, p, lng, lnb, cw, bs, cos, sin, dm, cd, kd, rg, st_in)


def _outproj_kernel(*refs):
    mix_ref, w_ref, x_ref, gt_ref, g2_ref, sc_ref, sh_ref, wrh_ref, wrl_ref, xo_ref, h_ref, lg_ref = refs
    y = _dot(mix_ref[...].astype(BF16), w_ref[...])
    xn = x_ref[...] + gt_ref[...] * y
    xo_ref[...] = xn
    h = _rms_mod(xn, g2_ref[...], sc_ref[...], sh_ref[...])
    h_ref[...] = h.reshape(h_ref.shape)
    hi = h.astype(BF16)
    lo = (h - hi.astype(F32)).astype(BF16)
    lg_ref[...] = _dot(hi, wrh_ref[...]) + (_dot(lo, wrh_ref[...]) + _dot(hi, wrl_ref[...]))


def _outproj_call(mix, w, x, gate, g2, sc, sh, wr, *, tm, name):
    m = x.shape[0]
    wr_hi = wr.astype(BF16)
    wr_lo = (wr - wr_hi.astype(F32)).astype(BF16)
    chunks = D // LANES

    def rows(a):
        if a.shape[0] == 1:
            return pl.BlockSpec((1, D), lambda i: (0, 0))
        return pl.BlockSpec((tm, D), lambda i: (i, 0))

    blk = pl.BlockSpec((tm, D), lambda i: (i, 0))
    in_specs = [blk, pl.BlockSpec((D, D), lambda i: (0, 0)), blk, rows(gate),
                pl.BlockSpec((1, D), lambda i: (0, 0)), rows(sc), rows(sh),
                pl.BlockSpec((D, LANES), lambda i: (0, 0)), pl.BlockSpec((D, LANES), lambda i: (0, 0))]
    return pl.pallas_call(
        _outproj_kernel,
        grid=(m // tm,),
        in_specs=in_specs,
        out_specs=[blk, pl.BlockSpec((tm, chunks, LANES), lambda i: (i, 0, 0)),
                   pl.BlockSpec((tm, LANES), lambda i: (i, 0))],
        out_shape=[jax.ShapeDtypeStruct((m, D), F32), jax.ShapeDtypeStruct((m, chunks, LANES), F32),
                   jax.ShapeDtypeStruct((m, LANES), F32)],
        compiler_params=_cparams(("parallel",)),
        name=name,
    )(mix, w, x, gate, g2, sc, sh, wr_hi, wr_lo)


def _route_kernel(lg_ref, bias_ref, tri_ref, eid_ref, gw_ref, rank_ref, cnt_ref, carry):
    i = pl.program_id(0)

    @pl.when(i == 0)
    def _():
        carry[...] = jnp.zeros_like(carry)

    tb = lg_ref.shape[1]
    s = _sigmoid(lg_ref[...])
    sel = s + bias_ref[...]
    epg = N_E // N_G
    r_sel = [sel[e:e + 1, :] for e in range(N_E)]
    r_s = [s[e:e + 1, :] for e in range(N_E)]

    gs = []
    for g in range(N_G):
        a, b, c, d = r_sel[epg * g: epg * g + 4]
        hi01, lo01 = jnp.maximum(a, b), jnp.minimum(a, b)
        hi23, lo23 = jnp.maximum(c, d), jnp.minimum(c, d)
        top1 = jnp.maximum(hi01, hi23)
        top2 = jnp.maximum(jnp.minimum(hi01, hi23), jnp.where(hi01 >= hi23, lo01, lo23))
        gs.append(top1 + top2)
    best = gs[0]
    gidx = jnp.zeros((1, tb), I32)
    for g in range(1, N_G):
        better = gs[g] > best
        gidx = jnp.where(better, g, gidx)
        best = jnp.where(better, gs[g], best)

    def pick_group(rws, j):
        out = rws[(N_G - 1) * epg + j]
        for g in range(N_G - 2, -1, -1):
            out = jnp.where(gidx == g, rws[g * epg + j], out)
        return out

    ing = [pick_group(r_sel, j) for j in range(epg)]
    sg = [pick_group(r_s, j) for j in range(epg)]

    def argmax_first(vals):
        bv, bi = vals[0], jnp.zeros((1, tb), I32)
        for j in range(1, epg):
            better = vals[j] > bv
            bi = jnp.where(better, j, bi)
            bv = jnp.where(better, vals[j], bv)
        return bi

    i1 = argmax_first(ing)
    i2 = argmax_first([jnp.where(i1 == j, -jnp.inf, ing[j]) for j in range(epg)])

    def pick_local(vals, idx):
        out = vals[epg - 1]
        for j in range(epg - 2, -1, -1):
            out = jnp.where(idx == j, vals[j], out)
        return out

    g1, g2 = pick_local(sg, i1), pick_local(sg, i2)
    den = g1 + g2
    e1 = gidx * epg + i1
    e2 = gidx * epg + i2
    eid_ref[0:1, :] = e1
    eid_ref[1:2, :] = e2
    gw_ref[0:1, :] = g1 / den
    gw_ref[1:2, :] = g2 / den

    eio = lax.broadcasted_iota(I32, (N_E, tb), 0)
    oh1 = eio == e1
    oh2 = eio == e2
    member = jnp.where(oh1, 1.0, jnp.where(oh2, 1.0, 0.0))
    before = carry[:, 0:1] + _dot(member.astype(BF16), tri_ref[...])
    rank_ref[0:1, :] = jnp.sum(jnp.where(oh1, before, 0.0), axis=0, keepdims=True).astype(I32)
    rank_ref[1:2, :] = jnp.sum(jnp.where(oh2, before, 0.0), axis=0, keepdims=True).astype(I32)
    carry[...] = carry[...] + jnp.sum(member, axis=1, keepdims=True)
    cnt_ref[...] = carry[...].astype(I32)


def _route_call(logits_t, bias, *, tb):
    t = logits_t.shape[1]
    tri = (jnp.arange(tb)[:, None] < jnp.arange(tb)[None, :]).astype(BF16)
    tok = pl.BlockSpec((2, tb), lambda i: (0, i))
    return pl.pallas_call(
        _route_kernel,
        grid=(t // tb,),
        in_specs=[pl.BlockSpec((N_E, tb), lambda i: (0, i)), pl.BlockSpec((N_E, 1), lambda i: (0, 0)),
                  pl.BlockSpec((tb, tb), lambda i: (0, 0))],
        out_specs=[tok, tok, tok, pl.BlockSpec((N_E, LANES), lambda i: (0, 0))],
        out_shape=[jax.ShapeDtypeStruct((2, t), I32), jax.ShapeDtypeStruct((2, t), F32),
                   jax.ShapeDtypeStruct((2, t), I32), jax.ShapeDtypeStruct((N_E, LANES), I32)],
        scratch_shapes=[pltpu.VMEM((N_E, LANES), F32)],
        compiler_params=_cparams(("arbitrary",)),
        name="route",
    )(logits_t, bias.reshape(N_E, 1), tri)


def _row_copy(src_ref, src_row, dst_ref, dst_row, sem):
    return pltpu.make_async_copy(src_ref.at[src_row], dst_ref.at[dst_row], sem)


def _dispatch_kernel(tok_ref, h_ref, o_ref, buf, sem, *, tg):
    i = pl.program_id(0)

    def start_block(blk, slot):
        base = blk * tg

        def issue(g, c):
            for u in range(ROW_DMA_UNROLL):
                r = g * ROW_DMA_UNROLL + u
                _row_copy(h_ref, tok_ref[base + r], buf.at[slot], r, sem.at[slot]).start(priority=u % 2)
            return c

        lax.fori_loop(0, tg // ROW_DMA_UNROLL, issue, 0)

    @pl.when(i == 0)
    def _():
        start_block(0, 0)

    @pl.when(i + 1 < pl.num_programs(0))
    def _():
        start_block(i + 1, (i + 1) & 1)

    slot = i & 1
    pltpu.make_async_copy(h_ref.at[pl.ds(0, tg)], buf.at[slot], sem.at[slot]).wait()
    o_ref[...] = buf[slot].reshape(tg, D).astype(BF16)


def _dispatch_call(slot_token, h, *, tg):
    n_slots = slot_token.shape[0]
    return pl.pallas_call(
        functools.partial(_dispatch_kernel, tg=tg),
        grid_spec=pltpu.PrefetchScalarGridSpec(
            num_scalar_prefetch=1,
            grid=(n_slots // tg,),
            in_specs=[pl.BlockSpec(memory_space=pl.ANY)],
            out_specs=pl.BlockSpec((tg, D), lambda i, tok: (i, 0)),
            scratch_shapes=[pltpu.VMEM((2, tg, D // LANES, LANES), F32), pltpu.SemaphoreType.DMA((2,))],
        ),
        out_shape=jax.ShapeDtypeStruct((n_slots, D), BF16),
        compiler_params=_cparams(("arbitrary",)),
        name="moe_dispatch",
    )(slot_token, h)


def _weight_tile_ring(sched, srcs, bufs, sem, layer, tn):
    col_ref, first_ref, tile_ref, te_ref, tc_ref, nt_ref = sched
    i = pl.program_id(0)
    t = tile_ref[i]

    def copies(tt, slot):
        col0 = pl.multiple_of(tc_ref[tt] * tn, tn)
        return [pltpu.make_async_copy(src.at[layer, te_ref[tt], :, pl.ds(col0, tn)], buf.at[slot], sem.at[slot])
                for src, buf in zip(srcs, bufs)]

    @pl.when(first_ref[i] == 1)
    def _():
        @pl.when(t == 0)
        def _():
            for c in copies(0, 0):
                c.start()

        @pl.when(t + 1 < nt_ref[0])
        def _():
            for c in copies(t + 1, (t + 1) & 1):
                c.start()

        for c in copies(t, t & 1):
            c.wait()

    return t & 1


def _gateup_kernel(col_ref, row_ref, valid_ref, first_ref, tile_ref, te_ref, tc_ref, nt_ref,
                   x_ref, wg_hbm, wu_hbm, o_ref, gbuf, ubuf, sem, *, layer, tn):
    slot = _weight_tile_ring((col_ref, first_ref, tile_ref, te_ref, tc_ref, nt_ref),
                             (wg_hbm, wu_hbm), (gbuf, ubuf), sem, layer, tn)

    @pl.when(valid_ref[pl.program_id(0)] == 1)
    def _():
        x = x_ref[...]
        g = lax.dot_general(x, gbuf[slot], (((1,), (0,)), ((), ())), preferred_element_type=F32)
        u = lax.dot_general(x, ubuf[slot], (((1,), (0,)), ((), ())), preferred_element_type=F32)
        o_ref[...] = ((g * _sigmoid(g)) * u).astype(BF16)

    @pl.when(valid_ref[pl.program_id(0)] == 0)
    def _():
        o_ref[...] = jnp.zeros_like(o_ref)


def _down_kernel(col_ref, row_ref, valid_ref, first_ref, tile_ref, te_ref, tc_ref, nt_ref,
                 h_ref, wd_hbm, o_ref, dbuf, sem, *, layer, tn):
    slot = _weight_tile_ring((col_ref, first_ref, tile_ref, te_ref, tc_ref, nt_ref),
                             (wd_hbm,), (dbuf,), sem, layer, tn)

    @pl.when(valid_ref[pl.program_id(0)] == 1)
    def _():
        y = lax.dot_general(h_ref[...], dbuf[slot], (((1,), (0,)), ((), ())), preferred_element_type=F32)
        o_ref[...] = y.reshape(o_ref.shape)

    @pl.when(valid_ref[pl.program_id(0)] == 0)
    def _():
        o_ref[...] = jnp.zeros_like(o_ref)


def _gateup_call(sched, x_sorted, wg, wu, layer, *, tm, tn):
    n_slots = x_sorted.shape[0]
    n_items = sched[0].shape[0]
    return pl.pallas_call(
        functools.partial(_gateup_kernel, layer=layer, tn=tn),
        grid_spec=pltpu.PrefetchScalarGridSpec(
            num_scalar_prefetch=len(sched),
            grid=(n_items,),
            in_specs=[pl.BlockSpec((tm, D), lambda i, col, row, *_: (row[i], 0)),
                      pl.BlockSpec(memory_space=pl.ANY), pl.BlockSpec(memory_space=pl.ANY)],
            out_specs=pl.BlockSpec((tm, tn), lambda i, col, row, *_: (row[i], col[i])),
            scratch_shapes=[pltpu.VMEM((2, D, tn), F32), pltpu.VMEM((2, D, tn), F32),
                            pltpu.SemaphoreType.DMA((2,))],
        ),
        out_shape=jax.ShapeDtypeStruct((n_slots, FF), BF16),
        compiler_params=_cparams(("arbitrary",)),
        name="moe_gateup",
    )(*sched, x_sorted, wg, wu)


def _down_call(sched, hs, wd, layer, *, tm, tn):
    n_slots = hs.shape[0]
    n_items = sched[0].shape[0]
    return pl.pallas_call(
        functools.partial(_down_kernel, layer=layer, tn=tn),
        grid_spec=pltpu.PrefetchScalarGridSpec(
            num_scalar_prefetch=len(sched),
            grid=(n_items,),
            in_specs=[pl.BlockSpec((tm, FF), lambda i, col, row, *_: (row[i], 0)),
                      pl.BlockSpec(memory_space=pl.ANY)],
            out_specs=pl.BlockSpec((tm, tn // LANES, LANES), lambda i, col, row, *_: (row[i], col[i], 0)),
            scratch_shapes=[pltpu.VMEM((2, FF, tn), F32), pltpu.SemaphoreType.DMA((2,))],
        ),
        out_shape=jax.ShapeDtypeStruct((n_slots, D // LANES, LANES), F32),
        compiler_params=_cparams(("arbitrary",)),
        name="moe_down",
    )(*sched, hs, wd)


def _moe_items(blocks_per_expert, n_tiles, n_blocks):
    bends = jnp.cumsum(blocks_per_expert)
    bstart = bends - blocks_per_expert
    total = bends[-1] * n_tiles
    step = jnp.arange(n_blocks * n_tiles, dtype=I32)
    j = jnp.minimum(step, total - 1)
    e = jnp.minimum(jnp.sum((j[:, None] >= (bends * n_tiles)[None, :]).astype(I32), axis=1), N_E - 1)
    local = j - bstart[e] * n_tiles
    nb = jnp.maximum(blocks_per_expert[e], 1)
    valid = step < total
    tail = jnp.maximum(step - total, 0)
    col = jnp.where(valid, local // nb, tail % n_tiles)
    row = jnp.where(valid, bstart[e] + local % nb, bends[-1] + tail // n_tiles)
    first = valid & (local % nb == 0)
    tile = jnp.maximum(jnp.cumsum(first.astype(I32)) - 1, 0)
    max_tiles = N_E * n_tiles
    where = jnp.where(first, tile, max_tiles)
    tile_e = jnp.zeros((max_tiles,), I32).at[where].set(e.astype(I32), mode="drop")
    tile_c = jnp.zeros((max_tiles,), I32).at[where].set(col.astype(I32), mode="drop")
    n_used = jnp.sum(first.astype(I32)).reshape(1)
    return (col.astype(I32), row.astype(I32), valid.astype(I32), first.astype(I32), tile.astype(I32),
            tile_e, tile_c, n_used)


def _combine_kernel(dest_ref, y_ref, x_ref, gm_ref, gw_ref, o_ref, buf, sem, *, tb, t_off, t_total):
    i = pl.program_id(0)

    def start_block(blk, slot):
        base = t_off + blk * tb

        def issue(g, c):
            for u in range(ROW_DMA_UNROLL // 2):
                r = g * (ROW_DMA_UNROLL // 2) + u
                for k in range(2):
                    _row_copy(y_ref, dest_ref[k * t_total + base + r], buf.at[slot, k], r,
                              sem.at[slot]).start(priority=k)
            return c

        lax.fori_loop(0, tb // (ROW_DMA_UNROLL // 2), issue, 0)

    @pl.when(i == 0)
    def _():
        start_block(0, 0)

    @pl.when(i + 1 < pl.num_programs(0))
    def _():
        start_block(i + 1, (i + 1) & 1)

    slot = i & 1
    for k in range(2):
        pltpu.make_async_copy(y_ref.at[pl.ds(0, tb)], buf.at[slot, k], sem.at[slot]).wait()
    gw = gw_ref[...]
    y0 = buf[slot, 0].reshape(tb, D)
    y1 = buf[slot, 1].reshape(tb, D)
    o_ref[...] = x_ref[...] + gm_ref[...] * (gw[:, 0:1] * y0 + gw[:, 1:2] * y1)


def _combine_call(dest_flat, y, x, gm, gw, *, tb, t_off, t_total, name):
    m = x.shape[0]
    if gm.shape[0] == 1:
        gm_spec = pl.BlockSpec((1, D), lambda i, d: (0, 0))
    else:
        gm_spec = pl.BlockSpec((tb, D), lambda i, d: (i, 0))
    return pl.pallas_call(
        functools.partial(_combine_kernel, tb=tb, t_off=t_off, t_total=t_total),
        grid_spec=pltpu.PrefetchScalarGridSpec(
            num_scalar_prefetch=1,
            grid=(m // tb,),
            in_specs=[pl.BlockSpec(memory_space=pl.ANY),
                      pl.BlockSpec((tb, D), lambda i, d: (i, 0)), gm_spec,
                      pl.BlockSpec((tb, 2), lambda i, d: (i, 0))],
            out_specs=pl.BlockSpec((tb, D), lambda i, d: (i, 0)),
            scratch_shapes=[pltpu.VMEM((2, 2, tb, D // LANES, LANES), F32), pltpu.SemaphoreType.DMA((2,))],
        ),
        out_shape=jax.ShapeDtypeStruct((m, D), F32),
        compiler_params=_cparams(("arbitrary",)),
        name=name,
    )(dest_flat, y, x, gm, gw)


def _moe(layer, h_all, logits_all, w_router_bias, wg, wu, wd, xp, xs, gmp, gms):
    t = h_all.shape[0]
    tm = MOE_TM
    eid, gw, rank, cnt = _route_call(logits_all[:, :N_E].T, w_router_bias, tb=640)
    counts = cnt[:, 0]
    blocks_per_expert = (counts + tm - 1) // tm
    pstart = (jnp.cumsum(blocks_per_expert) - blocks_per_expert) * tm
    first = jnp.sum(jnp.where(eid[None] == jnp.arange(N_E, dtype=I32)[:, None, None],
                              pstart[:, None, None], 0), axis=0)
    dest = first + rank
    n_blocks = (2 * t) // tm + N_E
    n_slots = n_blocks * tm
    tok = jnp.tile(jnp.arange(t, dtype=I32), 2)
    slot_token = jnp.zeros((n_slots,), I32).at[dest.reshape(-1)].set(tok)
    x_sorted = _dispatch_call(slot_token, h_all, tg=MOE_DISPATCH_ROWS)
    items1 = _moe_items(blocks_per_expert, FF // MOE_FF_TN, n_blocks)
    hs = _gateup_call(items1, x_sorted, wg, wu, layer, tm=tm, tn=MOE_FF_TN)
    items2 = _moe_items(blocks_per_expert, D // MOE_D_TN, n_blocks)
    y = _down_call(items2, hs, wd, layer, tm=tm, tn=MOE_D_TN)
    dest_flat = dest.reshape(-1)
    gwt = gw.T
    np_ = xp.shape[0]
    xp2 = _combine_call(dest_flat, y, xp, gmp, gwt[:np_], tb=256, t_off=0, t_total=t, name="moe_combine_p")
    xs2 = _combine_call(dest_flat, y, xs, gms, gwt[np_:], tb=128, t_off=np_, t_total=t, name="moe_combine_s")
    return xp2, xs2


def _attn_kernel(lam_ref, qt_ref, k_ref, vt_ref, g_ref, o_ref, m_scr, l_scr, acc_scr, sa_scr, sb_scr,
                 *, tq, out_scale):
    qi = pl.program_id(1)
    qt = qt_ref[0, 0]
    feat = lax.broadcasted_iota(I32, (C_VD, tq), 0)
    zero = jnp.zeros_like(qt)
    qq = jnp.concatenate([jnp.where(feat < C_HD, qt, zero), jnp.where(feat >= C_HD, qt, zero)], axis=1)
    m_scr[...] = jnp.full_like(m_scr, -jnp.inf)
    l_scr[...] = jnp.zeros_like(l_scr)
    acc_scr[...] = jnp.zeros_like(acc_scr)

    def scores(j, dst, masked):
        off = pl.multiple_of(j * tq, tq)
        st = _dot(k_ref[pl.ds(off, tq), :], qq)
        if masked:
            key = lax.broadcasted_iota(I32, (tq, 2 * tq), 0)
            col = lax.broadcasted_iota(I32, (tq, 2 * tq), 1)
            st = jnp.where(key <= jnp.where(col >= tq, col - tq, col), st, -jnp.inf)
        dst[...] = st

    def absorb(src, j):
        st = src[...]
        m_prev = m_scr[...]
        m_new = jnp.maximum(m_prev, jnp.max(st, axis=0, keepdims=True))
        alpha = jnp.exp2(m_prev - m_new)
        p = jnp.exp2(st - m_new[0:1, :])
        l_scr[...] = alpha * l_scr[...] + jnp.sum(p, axis=0, keepdims=True)
        acc_scr[...] = alpha[0:1, :] * acc_scr[...] + _dot(vt_ref[0, j], p.astype(BF16))
        m_scr[...] = m_new

    scores(qi, sa_scr, True)
    npair = lax.shift_right_logical(qi, 1)

    def pair(t, c):
        scores(2 * t, sb_scr, False)
        absorb(sa_scr, jnp.where(t == 0, qi, 2 * t - 1))
        scores(2 * t + 1, sa_scr, False)
        absorb(sb_scr, 2 * t)
        return c

    lax.fori_loop(0, npair, pair, 0)
    held = jnp.where(npair == 0, qi, 2 * npair - 1)

    @pl.when((qi & 1) == 1)
    def _():
        scores(qi - 1, sb_scr, False)
        absorb(sa_scr, held)
        absorb(sb_scr, qi - 1)

    @pl.when((qi & 1) == 0)
    def _():
        absorb(sa_scr, held)

    on = acc_scr[...] * (1.0 / l_scr[0:1, :])
    o = (on[:, 0:tq] - lam_ref[0] * on[:, tq:2 * tq]).T
    o = o * lax.rsqrt(jnp.mean(o * o, axis=-1, keepdims=True) + EPS) * g_ref[...] * out_scale
    o_ref[...] = o.astype(o_ref.dtype)


def _attn_call(lam, qt, k, vt, g, *, tq, out_scale):
    s = k.shape[0]
    return pl.pallas_call(
        functools.partial(_attn_kernel, tq=tq, out_scale=out_scale),
        grid=(C_H, s // tq),
        in_specs=[
            pl.BlockSpec(memory_space=pltpu.SMEM),
            pl.BlockSpec((1, 1, C_VD, tq), lambda h, i: (h, i, 0, 0)),
            pl.BlockSpec((s, C_VD), lambda h, i: (0, h)),
            pl.BlockSpec((1, s // tq, C_VD, tq), lambda h, i: (h, 0, 0, 0)),
            pl.BlockSpec((1, C_VD), lambda h, i: (0, 0)),
        ],
        out_specs=pl.BlockSpec((tq, C_VD), lambda h, i: (i, h)),
        out_shape=jax.ShapeDtypeStruct((s, C_H * C_VD), BF16),
        scratch_shapes=[pltpu.VMEM((SUBLANES, 2 * tq), F32), pltpu.VMEM((SUBLANES, 2 * tq), F32),
                        pltpu.VMEM((C_VD, 2 * tq), F32),
                        pltpu.VMEM((tq, 2 * tq), F32), pltpu.VMEM((tq, 2 * tq), F32)],
        compiler_params=_cparams(("parallel", "arbitrary")),
        name="diff_attn_prompt",
    )(lam, qt, k, vt, g)


def _sattn_kernel(pt_ref, lam_ref, wq_ref, bias_ref, *rest, out_scale, n_pg):
    k_refs, v_refs = rest[:n_pg], rest[n_pg:2 * n_pg]
    kn_ref, vn_ref, g_ref, o_ref, m_scr, l_scr, acc_scr = rest[2 * n_pg:]
    b = pl.program_id(0)
    p = pl.program_id(1)

    @pl.when(p == 0)
    def _():
        m_scr[...] = jnp.full_like(m_scr, -jnp.inf)
        l_scr[...] = jnp.zeros_like(l_scr)
        acc_scr[...] = jnp.zeros_like(acc_scr)

    wq = wq_ref[0]
    reps = kn_ref.shape[0] // LANES

    def update(s, vmat):
        m_prev = m_scr[...]
        m_new = jnp.maximum(m_prev, jnp.max(s, axis=1, keepdims=True))
        alpha = jnp.exp2(m_prev - m_new)
        pr = jnp.exp2(s - jnp.concatenate([m_new] * reps, axis=1))
        l_scr[...] = alpha * l_scr[...] + jnp.sum(pr, axis=1, keepdims=True)
        acc_scr[...] = alpha * acc_scr[...] + _dot(pr, vmat)
        m_scr[...] = m_new

    for i in range(n_pg):
        update(_dot_nt(wq, k_refs[i][0]) + bias_ref[...], v_refs[i][0])

    @pl.when(p == pl.num_programs(1) - 1)
    def _():
        n = kn_ref.shape[0]
        rows = 2 * DEC_S
        s = _dot_nt(wq, kn_ref[...])
        c = lax.broadcasted_iota(I32, (wq.shape[0], n), 0)
        col = lax.broadcasted_iota(I32, (wq.shape[0], n), 1)
        tok = lax.shift_right_logical(col, 4)
        ok = ((col & (C_H - 1)) == lax.shift_right_logical(c, 3)) \
            & (lax.shift_right_logical(tok, 2) == b) & ((tok & (DEC_S - 1)) <= (c & (DEC_S - 1)))
        update(jnp.where(ok, s, -jnp.inf), vn_ref[...])
        out = acc_scr[...] / l_scr[...]
        for h in range(C_H):
            blk = out[h * rows:(h + 1) * rows, :]
            o = blk[0:DEC_S, :] - lam_ref[0] * blk[DEC_S:rows, :]
            on = o * lax.rsqrt(jnp.mean(o * o, axis=-1, keepdims=True) + EPS) * g_ref[...] * out_scale
            o_ref[0, :, h * C_VD:(h + 1) * C_VD] = on


def _sattn_call(page_table, lam, wq, bias, ck, cv, kn, vn, g, *, out_scale, n_pg):
    nb, n_pages = page_table.shape
    nrow = wq.shape[1]
    prow = ck.shape[1]

    def page_spec(i):
        return pl.BlockSpec((1, prow, C_VD), lambda b, p, pt: (pt[b * n_pages + p * n_pg + i], 0, 0))

    const2 = lambda b, p, pt: (0, 0)
    return pl.pallas_call(
        functools.partial(_sattn_kernel, out_scale=out_scale, n_pg=n_pg),
        grid_spec=pltpu.PrefetchScalarGridSpec(
            num_scalar_prefetch=1,
            grid=(nb, n_pages // n_pg),
            in_specs=[
                pl.BlockSpec(memory_space=pltpu.SMEM),
                pl.BlockSpec((1, nrow, C_VD), lambda b, p, pt: (b, 0, 0)),
                pl.BlockSpec(bias.shape, const2),
                *[page_spec(i) for i in range(n_pg)],
                *[page_spec(i) for i in range(n_pg)],
                pl.BlockSpec(kn.shape, const2),
                pl.BlockSpec(vn.shape, const2),
                pl.BlockSpec((1, C_VD), const2),
            ],
            out_specs=pl.BlockSpec((1, DEC_S, C_H * C_VD), lambda b, p, pt: (b, 0, 0)),
            scratch_shapes=[pltpu.VMEM((nrow, LANES), F32), pltpu.VMEM((nrow, LANES), F32),
                            pltpu.VMEM((nrow, C_VD), F32)],
        ),
        out_shape=jax.ShapeDtypeStruct((nb, DEC_S, C_H * C_VD), F32),
        compiler_params=_cparams(("parallel", "arbitrary")),
        name="diff_attn_sample",
    )(page_table.reshape(-1), lam, wq, bias, *([ck] * n_pg), *([cv] * n_pg), kn, vn, g)


def _rope0_tables(pos):
    half = B_DK // 2
    freqs = 10000.0 ** (-jnp.arange(half, dtype=F32) / half)
    ang = pos.astype(F32)[:, None] * freqs[None, :]
    return jnp.cos(ang), jnp.sin(ang)


def _rope1_tables(pos):
    half = C_ROT // 2
    freqs = 500000.0 ** (-jnp.arange(half, dtype=F32) / half)
    ang = pos.astype(F32)[:, None] * freqs[None, :]
    cos, sin = jnp.cos(ang), jnp.sin(ang)
    n = pos.shape[0]
    pad = C_HD - C_ROT
    cos_t = jnp.concatenate([cos, cos, jnp.ones((n, pad), F32)], axis=1)
    sin_a = jnp.concatenate([-sin, jnp.zeros((n, half + pad), F32)], axis=1)
    sin_b = jnp.concatenate([jnp.zeros((n, half), F32), sin, jnp.zeros((n, pad), F32)], axis=1)
    rep = lambda t: jnp.tile(t, (1, LANES // C_HD))
    return rep(cos_t), rep(sin_a), rep(sin_b)


def _log_gamma():
    return jnp.log(1.0 - 2.0 ** (-5.0 - jnp.arange(B_H, dtype=F32)))


def _lanes_per_head(per_head):
    return jnp.repeat(per_head, B_DK, axis=1)


def kernel(x_prompt, x_sample, c_prompt, c_sample, state_ret, cache_k, cache_v, page_table, w_ada, b_ada, norm_mix, norm_ffn, w_in0, gmlp_ln_g, gmlp_ln_b, gmlp_w_s, gmlp_b_s, ret_norm_g, w_out0, w_in1, q_norm_g, k_norm_g, lambda_q1, lambda_k1, lambda_q2, lambda_k2, subln_g, w_out1, w_router, router_bias, w_gate, w_up, w_down):
    xp = x_prompt.reshape(SEQ, D)
    xs = x_sample.reshape(NS, D)
    pos_p = jnp.arange(SEQ)
    pos_s = PAST + jnp.arange(DEC_S)

    mc = 1 + DEC_B
    mc_pad = -(-mc // SUBLANES) * SUBLANES
    c_all = jnp.concatenate([c_prompt, c_sample, jnp.zeros((mc_pad - mc, D), F32)], axis=0)
    mods = _ada_call(c_all, w_ada, b_ada).reshape(2, mc_pad, 6, D)
    wr_pad = jnp.pad(w_router, ((0, 0), (0, LANES - N_E)))

    def mod(l, i):
        return mods[l, 0:1, i], jnp.repeat(mods[l, 1:mc, i], DEC_S, axis=0)

    l = 0
    (shp, shs), (scp, scs), (gtp, gts) = mod(l, 0), mod(l, 1), mod(l, 2)
    (sh2p, sh2s), (sc2p, sc2s), (gt2p, gt2s) = mod(l, 3), mod(l, 4), mod(l, 5)
    gmix = norm_mix[l].reshape(1, D)
    w_in = w_in0[0].astype(BF16)
    pp = _inproj_call(xp, gmix, scp, shp, w_in, mode="f32", tm=512, tn=1024, name="inproj0_p")[0]
    ps = _inproj_call(xs, gmix, scs, shs, w_in, mode="f32", tm=NS, tn=1024, name="inproj0_s")[0]

    log_g = _log_gamma()
    lng, lnb = gmlp_ln_g[0].reshape(1, A_W), gmlp_ln_b[0].reshape(1, A_W)
    rg = ret_norm_g[0].reshape(1, B_H * B_DK)
    ch = 128
    idx = jnp.arange(ch, dtype=F32)
    rel = idx[:, None] - idx[None, :]
    dec = jnp.where(rel[None] >= 0, jnp.exp(jnp.maximum(rel, 0.0)[None] * log_g[:, None, None]), 0.0)
    cd = _lanes_per_head(jnp.exp((idx + 1.0)[:, None] * log_g[None, :]))
    kd = _lanes_per_head(jnp.exp((ch - 1.0 - idx)[:, None] * log_g[None, :]))
    sdec = jnp.exp(ch * log_g)
    wm = jnp.where(jnp.tril(jnp.ones((ch, ch), dtype=bool))[None], gmlp_w_s[0], 0.0).astype(BF16)
    bs = jnp.repeat(gmlp_b_s[0].T, A_W // A_H, axis=1)
    cos0p, sin0p = _rope0_tables(pos_p)
    mix_p, ret_p = _mix0p_call(pp, lng, lnb, wm, bs, cos0p, sin0p, dec, cd, kd, rg, sdec)

    rows = 2 * DEC_S
    tloc = jnp.arange(rows) % DEC_S
    ws4 = gmlp_w_s[0][:, :DEC_S, :DEC_S]
    cw = []
    dm = []
    for d in range(DEC_S):
        src = tloc - d
        ok = src >= 0
        coef = jnp.where(ok[:, None], ws4[:, tloc, jnp.maximum(src, 0)].T, 0.0)
        cw.append(jnp.repeat(coef, A_W // A_H, axis=1))
        dm.append(_lanes_per_head(jnp.where(ok[:, None], jnp.exp(float(d) * log_g)[None, :], 0.0)))
    cw, dm = jnp.stack(cw), jnp.stack(dm)
    bs_s = jnp.repeat(gmlp_b_s[0][:, :DEC_S].T[tloc], A_W // A_H, axis=1)
    tl = tloc.astype(F32)
    cd_s = _lanes_per_head(jnp.exp((tl + 1.0)[:, None] * log_g[None, :]))
    kd_s = _lanes_per_head(jnp.exp((DEC_S - 1.0 - tl)[:, None] * log_g[None, :]))
    sdec_s = jnp.exp(DEC_S * log_g)
    cos0s, sin0s = _rope0_tables(pos_s)
    mix_s, gv_s, ret_s = _mix0s_call(ps, state_ret[0], lng, lnb, cw, bs_s, cos0s[tloc], sin0s[tloc],
                                     dm, cd_s, kd_s, rg, sdec_s)

    gffn = norm_ffn[l].reshape(1, D)
    w_out = w_out0[0].astype(BF16)
    xp, h2p, lgp = _outproj_call(mix_p, w_out, xp, gtp, gffn, sc2p, sh2p, wr_pad, tm=256, name="outproj0_p")
    xs, h2s, lgs = _outproj_call(mix_s, w_out, xs, gts, gffn, sc2s, sh2s, wr_pad, tm=NS, name="outproj0_s")
    xp, xs = _moe(l, jnp.concatenate([h2p, h2s], axis=0), jnp.concatenate([lgp, lgs], axis=0), router_bias,
                  w_gate, w_up, w_down, xp, xs, gt2p, gt2s)

    l = 1
    (shp, shs), (scp, scs), (gtp, gts) = mod(l, 0), mod(l, 1), mod(l, 2)
    (sh2p, sh2s), (sc2p, sc2s), (gt2p, gt2s) = mod(l, 3), mod(l, 4), mod(l, 5)
    gmix = norm_mix[l].reshape(1, D)
    lambda_init = 0.8 - 0.6 * math.exp(-0.3 * l)
    lam = (jnp.exp(jnp.sum(lambda_q1[0] * lambda_k1[0])) - jnp.exp(jnp.sum(lambda_q2[0] * lambda_k2[0]))
           + lambda_init).reshape(1).astype(F32)
    hw = C_H * C_VD
    wq_w = w_in1[0][:, 0:hw].astype(BF16)
    wk_w = w_in1[0][:, hw:2 * hw].astype(BF16)
    wv_w = w_in1[0][:, 2 * hw:3 * hw].astype(BF16)
    gq = jnp.tile(q_norm_g[0], LANES // C_HD).reshape(1, LANES)
    gk = jnp.tile(k_norm_g[0], LANES // C_HD).reshape(1, LANES)
    lane = jnp.arange(LANES)
    bd = (lane[:, None] // C_HD == lane[None, :] // C_HD).astype(BF16)
    tabs_p = _rope1_tables(pos_p)
    tabs_s = tuple(jnp.tile(t, (DEC_B, 1)) for t in _rope1_tables(pos_s))
    qscale = C_HD ** -0.5 * math.log2(math.e)

    q_p = _inproj_call(xp, gmix, scp, shp, wq_w, mode="q", tm=512, tn=1024, extras=(gq, *tabs_p, bd),
                       scale=qscale, name="inproj1_q_p")[0]
    k_p, k_pb = _inproj_call(xp, gmix, scp, shp, wk_w, mode="k", tm=512, tn=1024, extras=(gk, *tabs_p, bd),
                             name="inproj1_k_p")
    v_p, v_pb = _inproj_call(xp, gmix, scp, shp, wv_w, mode="dual", tm=512, tn=1024, name="inproj1_v_p")
    q_s = _inproj_call(xs, gmix, scs, shs, wq_w, mode="q", tm=NS, tn=1024, extras=(gq, *tabs_s, bd),
                       scale=qscale, name="inproj1_q_s")[0]
    k_s, k_sb = _inproj_call(xs, gmix, scs, shs, wk_w, mode="k", tm=NS, tn=1024, extras=(gk, *tabs_s, bd),
                             name="inproj1_k_s")
    v_s, v_sb = _inproj_call(xs, gmix, scs, shs, wv_w, mode="dual", tm=NS, tn=1024, name="inproj1_v_s")

    gsub = subln_g[0].reshape(1, C_VD)
    out_scale = 1.0 - lambda_init
    nqb = SEQ // ATT_TQ
    to_tiles = lambda t: t.reshape(nqb, ATT_TQ, C_H, C_VD).transpose(2, 0, 3, 1)
    att_p = _attn_call(lam, to_tiles(q_p), k_pb, to_tiles(v_pb), gsub, tq=ATT_TQ, out_scale=out_scale)

    q5 = q_s.astype(F32).reshape(DEC_B, DEC_S, C_H, 2, C_HD).transpose(0, 2, 3, 1, 4)
    wq_rows = (q5[:, :, :, :, None, :] * jnp.eye(2, dtype=F32)[None, None, :, None, :, None]
               ).reshape(DEC_B, C_H * 2 * DEC_S, C_VD)
    n_pool = cache_k.shape[1]
    prow = PAGE * C_H
    srow = jnp.arange(C_H * 2 * DEC_S)[:, None] // (2 * DEC_S)
    head_bias = jnp.where(jnp.arange(prow)[None, :] % C_H == srow, 0.0, -jnp.inf).astype(F32)
    att_s = _sattn_call(page_table, lam, wq_rows, head_bias,
                        cache_k.reshape(n_pool, prow, C_VD), cache_v.reshape(n_pool, prow, C_VD),
                        k_s.reshape(NS * C_H, C_VD), v_s.reshape(NS * C_H, C_VD), gsub,
                        out_scale=out_scale, n_pg=SATT_PAGES)
    att_s = att_s.reshape(NS, hw)

    gffn = norm_ffn[l].reshape(1, D)
    w_out = w_out1[0].astype(BF16)
    xp, h2p, lgp = _outproj_call(att_p, w_out, xp, gtp, gffn, sc2p, sh2p, wr_pad, tm=256, name="outproj1_p")
    xs, h2s, lgs = _outproj_call(att_s, w_out, xs, gts, gffn, sc2s, sh2s, wr_pad, tm=NS, name="outproj1_s")
    xp, xs = _moe(l, jnp.concatenate([h2p, h2s], axis=0), jnp.concatenate([lgp, lgs], axis=0), router_bias,
                  w_gate, w_up, w_down, xp, xs, gt2p, gt2s)

    return (xp.reshape(1, SEQ, D), xs.reshape(DEC_B, DEC_S, D),
            ret_p.reshape(1, 1, B_H, B_DK, B_DK), ret_s.reshape(1, DEC_B, B_H, B_DK, B_DK),
            gv_s.reshape(1, DEC_B, DEC_S, A_W),
            k_p.reshape(1, 1, SEQ, C_H, C_VD), v_p.reshape(1, 1, SEQ, C_H, C_VD),
            k_s.reshape(1, DEC_B, DEC_S, C_H, C_VD), v_s.reshape(1, DEC_B, DEC_S, C_H, C_VD))
```

```python
import functools
import math

import jax
import jax.numpy as jnp
from jax import lax
from jax.experimental import pallas as pl
from jax.experimental.pallas import tpu as pltpu

F32 = jnp.float32
BF16 = jnp.bfloat16
I32 = jnp.int32

D = 2048
SEQ = 8192
DEC_B = 32
DEC_S = 4
NS = DEC_B * DEC_S
PAST = 8192
PAGE = 128
EPS = 1e-6
A_W = 1024
A_H = 8
B_H = 4
B_DK = 256
C_H = 16
C_HD = 64
C_VD = 128
C_ROT = 16
N_E = 16
N_G = 4
FF = 1536

LANES = 128
SUBLANES = 8
BF16_SUBLANES = 16
VMEM_LIMIT_BYTES = 56 * 1024 * 1024

PROJ_TM = 1024
PROJ1_TM = 512
OUTPROJ_TM = 512
MOE_TM = 256
MOE_FF_TN = 768
MOE_D_TN = 1024
MOE_DISPATCH_ROWS = 3 * MOE_TM
ROW_DMA_UNROLL = 8
ATT_TQ = 512
SATT_PAGES = 8


def _cparams(sem):
    return pltpu.CompilerParams(dimension_semantics=sem, vmem_limit_bytes=VMEM_LIMIT_BYTES)


def _sigmoid(x):
    return 1.0 / (1.0 + jnp.exp(-x))


def _gelu(x):
    return x * (0.5 * (1.0 + jnp.tanh(0.7978845608028654 * (x + 0.044715 * (x * x * x)))))


def _rms_mod(x, g, sc, sh):
    ms = jnp.mean(x * x, axis=-1, keepdims=True)
    return (x * lax.rsqrt(ms + EPS)) * g * (1.0 + sc) + sh


def _dot(a, b):
    return jnp.dot(a, b, preferred_element_type=F32)


def _dot_nt(a, b):
    return lax.dot_general(a, b, (((1,), (1,)), ((), ())), preferred_element_type=F32)


def _dot_tn(a, b):
    return lax.dot_general(a, b, (((0,), (0,)), ((), ())), preferred_element_type=F32)


def _split_bf16(a):
    hi = a.astype(BF16)
    return hi, (a - hi.astype(F32)).astype(BF16)


def _ada_kernel(c_ref, w_ref, b_ref, o_ref):
    c = c_ref[...]
    s_hi, s_lo = _split_bf16(c * _sigmoid(c))
    w_hi, w_lo = _split_bf16(w_ref[0])
    o_ref[0] = (_dot(s_hi, w_hi) + (_dot(s_lo, w_hi) + _dot(s_hi, w_lo))) + b_ref[0]


def _ada_call(c_all, w_ada, b_ada):
    nl, _, n = w_ada.shape
    mc = c_all.shape[0]
    tn = 1024
    return pl.pallas_call(
        _ada_kernel,
        grid=(nl, n // tn),
        in_specs=[
            pl.BlockSpec((mc, D), lambda l, j: (0, 0)),
            pl.BlockSpec((1, D, tn), lambda l, j: (l, 0, j)),
            pl.BlockSpec((1, 1, tn), lambda l, j: (l, 0, j)),
        ],
        out_specs=pl.BlockSpec((1, mc, tn), lambda l, j: (l, 0, j)),
        out_shape=jax.ShapeDtypeStruct((nl, mc, n), F32),
        compiler_params=_cparams(("parallel", "parallel")),
        name="ada",
    )(c_all, w_ada, b_ada.reshape(nl, 1, n))


def _inproj_kernel(x_ref, g_ref, sc_ref, sh_ref, w_ref, o_ref, h_scr):
    @pl.when(pl.program_id(1) == 0)
    def _():
        h_scr[...] = _rms_mod(x_ref[...], g_ref[...], sc_ref[...], sh_ref[...]).astype(BF16)

    o_ref[...] = _dot(h_scr[...], w_ref[...])


def _inproj_call(x, g, sc, sh, w, *, tm, tn, name):
    m = x.shape[0]
    n = w.shape[1]

    def rows(a):
        if a.shape[0] == 1:
            return pl.BlockSpec((1, D), lambda i, j: (0, 0))
        return pl.BlockSpec((tm, D), lambda i, j: (i, 0))

    return pl.pallas_call(
        _inproj_kernel,
        grid=(m // tm, n // tn),
        in_specs=[pl.BlockSpec((tm, D), lambda i, j: (i, 0)), pl.BlockSpec((1, D), lambda i, j: (0, 0)),
                  rows(sc), rows(sh), pl.BlockSpec((D, tn), lambda i, j: (0, j))],
        out_specs=pl.BlockSpec((tm, tn), lambda i, j: (i, j)),
        out_shape=jax.ShapeDtypeStruct((m, n), F32),
        scratch_shapes=[pltpu.VMEM((tm, D), BF16)],
        compiler_params=_cparams(("parallel", "arbitrary")),
        name=name,
    )(x, g, sc, sh, w)


def _qk_epilogue(acc, gain, cos_t, sin_a, sin_b, bd, scale, emit):
    for j in range(acc.shape[1] // LANES):
        sl = slice(j * LANES, (j + 1) * LANES)
        y = acc[:, sl]
        hi, lo = _split_bf16(y * y)
        ss = _dot(hi, bd) + _dot(lo, bd)
        yn = y * lax.rsqrt(ss * (1.0 / C_HD) + EPS) * gain
        r = yn * cos_t + pltpu.roll(yn, LANES - C_ROT // 2, 1) * sin_a + pltpu.roll(yn, C_ROT // 2, 1) * sin_b
        emit(sl, r * scale if scale != 1.0 else r)


def _inproj1_kernel(x_ref, g_ref, sc_ref, sh_ref, w_ref, gq_ref, gk_ref, cos_ref, sa_ref, sb_ref, bd_ref,
                    q_ref, kf_ref, kb_ref, vf_ref, vb_ref, h_scr, *, qscale, tiles):
    j = pl.program_id(1)

    @pl.when(j == 0)
    def _():
        h_scr[...] = _rms_mod(x_ref[...], g_ref[...], sc_ref[...], sh_ref[...]).astype(BF16)

    acc = _dot(h_scr[...], w_ref[...])
    tabs = (cos_ref[...], sa_ref[...], sb_ref[...], bd_ref[...])

    @pl.when(j < tiles)
    def _():
        def emit(sl, r):
            q_ref[:, sl] = r.astype(BF16)
        _qk_epilogue(acc, gq_ref[...], *tabs, qscale, emit)

    @pl.when((j >= tiles) & (j < 2 * tiles))
    def _():
        def emit(sl, r):
            kf_ref[:, sl] = r
            kb_ref[:, sl] = r.astype(BF16)
        _qk_epilogue(acc, gk_ref[...], *tabs, 1.0, emit)

    @pl.when(j >= 2 * tiles)
    def _():
        vf_ref[...] = acc
        vb_ref[...] = acc.astype(BF16)


def _inproj1_call(x, g, sc, sh, w, gq, gk, cos_t, sin_a, sin_b, bd, *, tm, tn, qscale, name):
    m = x.shape[0]
    width = w.shape[1] // 3
    tiles = width // tn

    def rows(a):
        if a.shape[0] == 1:
            return pl.BlockSpec((1, D), lambda i, j: (0, 0))
        return pl.BlockSpec((tm, D), lambda i, j: (i, 0))

    tab = pl.BlockSpec((tm, LANES), lambda i, j: (i, 0))
    one = pl.BlockSpec((1, LANES), lambda i, j: (0, 0))

    def out_spec(part):
        return pl.BlockSpec((tm, tn), lambda i, j: (i, jnp.clip(j - part * tiles, 0, tiles - 1)))

    f32o = jax.ShapeDtypeStruct((m, width), F32)
    bf16o = jax.ShapeDtypeStruct((m, width), BF16)
    return pl.pallas_call(
        functools.partial(_inproj1_kernel, qscale=qscale, tiles=tiles),
        grid=(m // tm, 3 * tiles),
        in_specs=[pl.BlockSpec((tm, D), lambda i, j: (i, 0)), pl.BlockSpec((1, D), lambda i, j: (0, 0)),
                  rows(sc), rows(sh), pl.BlockSpec((D, tn), lambda i, j: (0, j)),
                  one, one, tab, tab, tab, pl.BlockSpec((LANES, LANES), lambda i, j: (0, 0))],
        out_specs=[out_spec(0), out_spec(1), out_spec(1), out_spec(2), out_spec(2)],
        out_shape=[bf16o, f32o, bf16o, f32o, bf16o],
        scratch_shapes=[pltpu.VMEM((tm, D), BF16)],
        compiler_params=_cparams(("parallel", "arbitrary")),
        name=name,
    )(x, g, sc, sh, w, gq, gk, cos_t, sin_a, sin_b, bd)


def _mix0p_kernel(sdec_ref, p_ref, lng_ref, lnb_ref, wm_ref, bs_ref, cos_ref, sin_ref, dec_ref,
                  cd_ref, kd_ref, rg_ref, mix_ref, so_ref, st_scr):
    c = pl.program_id(0)

    @pl.when(c == 0)
    def _():
        st_scr[...] = jnp.zeros_like(st_scr)

    u = _gelu(p_ref[:, 0:A_W])
    v = _gelu(p_ref[:, A_W:2 * A_W])
    mu = jnp.mean(v, axis=-1, keepdims=True)
    vc = v - mu
    v = vc * lax.rsqrt(jnp.mean(vc * vc, axis=-1, keepdims=True) + EPS) * lng_ref[...] + lnb_ref[...]
    vb = v.astype(BF16)
    for hh in range(A_H):
        sl = slice(hh * LANES, (hh + 1) * LANES)
        sv = _dot(wm_ref[hh], vb[:, sl]) + bs_ref[:, sl]
        mix_ref[:, sl] = (u[:, sl] * sv).astype(BF16)

    cos, sin = cos_ref[...], sin_ref[...]
    half = B_DK // 2

    def rope(t):
        t1, t2 = t[:, :half], t[:, half:]
        return jnp.concatenate([t1 * cos - t2 * sin, t2 * cos + t1 * sin], axis=1)

    base = 2 * A_W
    hw = B_H * B_DK
    for hd in range(B_H):
        sl = slice(hd * B_DK, (hd + 1) * B_DK)
        q = rope(p_ref[:, base + hd * B_DK: base + (hd + 1) * B_DK])
        k = rope(p_ref[:, base + hw + hd * B_DK: base + hw + (hd + 1) * B_DK]) * (B_DK ** -0.5)
        vr = p_ref[:, base + 2 * hw + hd * B_DK: base + 2 * hw + (hd + 1) * B_DK]
        g = p_ref[:, base + 3 * hw + hd * B_DK: base + 3 * hw + (hd + 1) * B_DK]
        qb, kb, vrb = q.astype(BF16), k.astype(BF16), vr.astype(BF16)
        inner = _dot_nt(qb, kb) * dec_ref[hd]
        st = st_scr[hd]
        o = _dot(inner.astype(BF16), vrb) + _dot(qb, st.astype(BF16)) * cd_ref[:, sl]
        kdb = (k * kd_ref[:, sl]).astype(BF16)
        st_scr[hd] = sdec_ref[hd] * st + _dot_tn(kdb, vrb)
        on = o * lax.rsqrt(jnp.mean(o * o, axis=-1, keepdims=True) + EPS) * rg_ref[:, sl]
        mix_ref[:, A_W + hd * B_DK: A_W + (hd + 1) * B_DK] = ((g * _sigmoid(g)) * on).astype(BF16)

    @pl.when(c == pl.num_programs(0) - 1)
    def _():
        so_ref[...] = st_scr[...]


def _mix0p_call(p, lng, lnb, wm, bs, cos, sin, dec, cd, kd, rg, sdec):
    m = p.shape[0]
    ch = 128
    full = lambda shape: pl.BlockSpec(shape, lambda c: (0,) * len(shape))
    return pl.pallas_call(
        _mix0p_kernel,
        grid=(m // ch,),
        in_specs=[
            pl.BlockSpec(memory_space=pltpu.SMEM),
            pl.BlockSpec((ch, p.shape[1]), lambda c: (c, 0)),
            full((1, A_W)), full((1, A_W)),
            full((A_H, ch, ch)), full((ch, A_W)),
            pl.BlockSpec((ch, LANES), lambda c: (c, 0)),
            pl.BlockSpec((ch, LANES), lambda c: (c, 0)),
            full((B_H, ch, ch)), full((ch, B_H * B_DK)), full((ch, B_H * B_DK)),
            full((1, B_H * B_DK)),
        ],
        out_specs=[
            pl.BlockSpec((ch, D), lambda c: (c, 0)),
            full((B_H, B_DK, B_DK)),
        ],
        out_shape=[jax.ShapeDtypeStruct((m, D), BF16), jax.ShapeDtypeStruct((B_H, B_DK, B_DK), F32)],
        scratch_shapes=[pltpu.VMEM((B_H, B_DK, B_DK), F32)],
        compiler_params=_cparams(("arbitrary",)),
        name="mix0_prompt",
    )(sdec, p, lng, lnb, wm, bs, cos, sin, dec, cd, kd, rg)


def _mix0s_kernel(sdec_ref, p_ref, lng_ref, lnb_ref, cw_ref, bs_ref, cos_ref, sin_ref, dm_ref,
                  cd_ref, kd_ref, rg_ref, st_ref, mix_ref, gv_ref, so_ref):
    rows = 2 * DEC_S

    def shift_rows(t, d):
        return t if d == 0 else pltpu.roll(t, d, 0)

    u = _gelu(p_ref[:, 0:A_W])
    v = _gelu(p_ref[:, A_W:2 * A_W])
    mu = jnp.mean(v, axis=-1, keepdims=True)
    vc = v - mu
    v = vc * lax.rsqrt(jnp.mean(vc * vc, axis=-1, keepdims=True) + EPS) * lng_ref[...] + lnb_ref[...]
    gv_ref[...] = v
    sv = bs_ref[...]
    for d in range(DEC_S):
        sv = sv + cw_ref[d] * shift_rows(v, d)
    mix_ref[:, 0:A_W] = u * sv

    cos, sin = cos_ref[...], sin_ref[...]
    half = B_DK // 2

    def rope(t):
        t1, t2 = t[:, :half], t[:, half:]
        return jnp.concatenate([t1 * cos - t2 * sin, t2 * cos + t1 * sin], axis=1)

    grp = lax.shift_right_logical(lax.broadcasted_iota(I32, (rows, B_DK), 0), 2)
    base = 2 * A_W
    hw = B_H * B_DK
    for hd in range(B_H):
        sl = slice(hd * B_DK, (hd + 1) * B_DK)
        q = rope(p_ref[:, base + hd * B_DK: base + (hd + 1) * B_DK])
        k = rope(p_ref[:, base + hw + hd * B_DK: base + hw + (hd + 1) * B_DK]) * (B_DK ** -0.5)
        vr = p_ref[:, base + 2 * hw + hd * B_DK: base + 2 * hw + (hd + 1) * B_DK]
        g = p_ref[:, base + 3 * hw + hd * B_DK: base + 3 * hw + (hd + 1) * B_DK]
        o = jnp.zeros((rows, B_DK), F32)
        for d in range(DEC_S):
            s = jnp.sum(q * shift_rows(k, d), axis=-1, keepdims=True)
            o = o + (s * dm_ref[d][:, sl]) * shift_rows(vr, d)
        kd = k * kd_ref[:, sl]
        for bi in range(2):
            st = st_ref[bi, hd]
            mine = grp == bi
            o = o + jnp.where(mine, _dot(q, st) * cd_ref[:, sl], 0.0)
            so_ref[bi, hd] = sdec_ref[hd] * st + _dot_tn(jnp.where(mine, kd, 0.0), vr)
        on = o * lax.rsqrt(jnp.mean(o * o, axis=-1, keepdims=True) + EPS) * rg_ref[:, sl]
        mix_ref[:, A_W + hd * B_DK: A_W + (hd + 1) * B_DK] = (g * _sigmoid(g)) * on


def _mix0s_call(p, st_in, lng, lnb, cw, bs, cos, sin, dm, cd, kd, rg, sdec):
    rows = 2 * DEC_S
    nb = st_in.shape[0]
    full = lambda shape: pl.BlockSpec(shape, lambda c: (0,) * len(shape))
    return pl.pallas_call(
        _mix0s_kernel,
        grid=(nb // 2,),
        in_specs=[
            pl.BlockSpec(memory_space=pltpu.SMEM),
            pl.BlockSpec((rows, p.shape[1]), lambda c: (c, 0)),
            full((1, A_W)), full((1, A_W)),
            full((DEC_S, rows, A_W)), full((rows, A_W)),
            full((rows, LANES)), full((rows, LANES)),
            full((DEC_S, rows, B_H * B_DK)), full((rows, B_H * B_DK)), full((rows, B_H * B_DK)),
            full((1, B_H * B_DK)),
            pl.BlockSpec((2, B_H, B_DK, B_DK), lambda c: (c, 0, 0, 0)),
        ],
        out_specs=[
            pl.BlockSpec((rows, D), lambda c: (c, 0)),
            pl.BlockSpec((rows, A_W), lambda c: (c, 0)),
            pl.BlockSpec((2, B_H, B_DK, B_DK), lambda c: (c, 0, 0, 0)),
        ],
        out_shape=[
            jax.ShapeDtypeStruct((nb * DEC_S, D), F32),
            jax.ShapeDtypeStruct((nb * DEC_S, A_W), F32),
            jax.ShapeDtypeStruct(st_in.shape, F32),
        ],
        compiler_params=_cparams(("parallel",)),
        name="mix0_sample",
    )(sdec, p, lng, lnb, cw, bs, cos, sin, dm, cd, kd, rg, st_in)


def _outproj_kernel(*refs):
    mix_ref, w_ref, x_ref, gt_ref, g2_ref, sc_ref, sh_ref, wrh_ref, wrl_ref, xo_ref, h_ref, lg_ref = refs
    y = _dot(mix_ref[...].astype(BF16), w_ref[...])
    xn = x_ref[...] + gt_ref[...] * y
    xo_ref[...] = xn
    h = _rms_mod(xn, g2_ref[...], sc_ref[...], sh_ref[...])
    h_ref[...] = h.reshape(h_ref.shape).astype(BF16)
    hi = h.astype(BF16)
    lo = (h - hi.astype(F32)).astype(BF16)
    lg_ref[...] = _dot(hi, wrh_ref[...]) + (_dot(lo, wrh_ref[...]) + _dot(hi, wrl_ref[...]))


def _outproj_call(mix, w, x, gate, g2, sc, sh, wr, *, tm, name):
    m = x.shape[0]
    wr_hi = wr.astype(BF16)
    wr_lo = (wr - wr_hi.astype(F32)).astype(BF16)
    chunks = D // LANES

    def rows(a):
        if a.shape[0] == 1:
            return pl.BlockSpec((1, D), lambda i: (0, 0))
        return pl.BlockSpec((tm, D), lambda i: (i, 0))

    blk = pl.BlockSpec((tm, D), lambda i: (i, 0))
    in_specs = [blk, pl.BlockSpec((D, D), lambda i: (0, 0)), blk, rows(gate),
                pl.BlockSpec((1, D), lambda i: (0, 0)), rows(sc), rows(sh),
                pl.BlockSpec((D, LANES), lambda i: (0, 0)), pl.BlockSpec((D, LANES), lambda i: (0, 0))]
    return pl.pallas_call(
        _outproj_kernel,
        grid=(m // tm,),
        in_specs=in_specs,
        out_specs=[blk, pl.BlockSpec((tm, chunks, LANES), lambda i: (i, 0, 0)),
                   pl.BlockSpec((tm, LANES), lambda i: (i, 0))],
        out_shape=[jax.ShapeDtypeStruct((m, D), F32), jax.ShapeDtypeStruct((m, chunks, LANES), BF16),
                   jax.ShapeDtypeStruct((m, LANES), F32)],
        compiler_params=_cparams(("parallel",)),
        name=name,
    )(mix, w, x, gate, g2, sc, sh, wr_hi, wr_lo)


def _route_kernel(lg_ref, bias_ref, tri_ref, eid_ref, gw_ref, rank_ref, cnt_ref, carry):
    i = pl.program_id(0)

    @pl.when(i == 0)
    def _():
        carry[...] = jnp.zeros_like(carry)

    tb = lg_ref.shape[1]
    s = _sigmoid(lg_ref[...])
    sel = s + bias_ref[...]
    epg = N_E // N_G
    r_sel = [sel[e:e + 1, :] for e in range(N_E)]
    r_s = [s[e:e + 1, :] for e in range(N_E)]

    gs = []
    for g in range(N_G):
        a, b, c, d = r_sel[epg * g: epg * g + 4]
        hi01, lo01 = jnp.maximum(a, b), jnp.minimum(a, b)
        hi23, lo23 = jnp.maximum(c, d), jnp.minimum(c, d)
        top1 = jnp.maximum(hi01, hi23)
        top2 = jnp.maximum(jnp.minimum(hi01, hi23), jnp.where(hi01 >= hi23, lo01, lo23))
        gs.append(top1 + top2)
    best = gs[0]
    gidx = jnp.zeros((1, tb), I32)
    for g in range(1, N_G):
        better = gs[g] > best
        gidx = jnp.where(better, g, gidx)
        best = jnp.where(better, gs[g], best)

    def pick_group(rws, j):
        out = rws[(N_G - 1) * epg + j]
        for g in range(N_G - 2, -1, -1):
            out = jnp.where(gidx == g, rws[g * epg + j], out)
        return out

    ing = [pick_group(r_sel, j) for j in range(epg)]
    sg = [pick_group(r_s, j) for j in range(epg)]

    def argmax_first(vals):
        bv, bi = vals[0], jnp.zeros((1, tb), I32)
        for j in range(1, epg):
            better = vals[j] > bv
            bi = jnp.where(better, j, bi)
            bv = jnp.where(better, vals[j], bv)
        return bi

    i1 = argmax_first(ing)
    i2 = argmax_first([jnp.where(i1 == j, -jnp.inf, ing[j]) for j in range(epg)])

    def pick_local(vals, idx):
        out = vals[epg - 1]
        for j in range(epg - 2, -1, -1):
            out = jnp.where(idx == j, vals[j], out)
        return out

    g1, g2 = pick_local(sg, i1), pick_local(sg, i2)
    den = g1 + g2
    e1 = gidx * epg + i1
    e2 = gidx * epg + i2
    eid_ref[0:1, :] = e1
    eid_ref[1:2, :] = e2
    gw_ref[0:1, :] = g1 / den
    gw_ref[1:2, :] = g2 / den

    eio = lax.broadcasted_iota(I32, (N_E, tb), 0)
    oh1 = eio == e1
    oh2 = eio == e2
    member = jnp.where(oh1, 1.0, jnp.where(oh2, 1.0, 0.0))
    before = carry[:, 0:1] + _dot(member.astype(BF16), tri_ref[...])
    rank_ref[0:1, :] = jnp.sum(jnp.where(oh1, before, 0.0), axis=0, keepdims=True).astype(I32)
    rank_ref[1:2, :] = jnp.sum(jnp.where(oh2, before, 0.0), axis=0, keepdims=True).astype(I32)
    carry[...] = carry[...] + jnp.sum(member, axis=1, keepdims=True)
    cnt_ref[...] = carry[...].astype(I32)


def _route_call(logits_t, bias, *, tb):
    t = logits_t.shape[1]
    tri = (jnp.arange(tb)[:, None] < jnp.arange(tb)[None, :]).astype(BF16)
    tok = pl.BlockSpec((2, tb), lambda i: (0, i))
    return pl.pallas_call(
        _route_kernel,
        grid=(t // tb,),
        in_specs=[pl.BlockSpec((N_E, tb), lambda i: (0, i)), pl.BlockSpec((N_E, 1), lambda i: (0, 0)),
                  pl.BlockSpec((tb, tb), lambda i: (0, 0))],
        out_specs=[tok, tok, tok, pl.BlockSpec((N_E, LANES), lambda i: (0, 0))],
        out_shape=[jax.ShapeDtypeStruct((2, t), I32), jax.ShapeDtypeStruct((2, t), F32),
                   jax.ShapeDtypeStruct((2, t), I32), jax.ShapeDtypeStruct((N_E, LANES), I32)],
        scratch_shapes=[pltpu.VMEM((N_E, LANES), F32)],
        compiler_params=_cparams(("arbitrary",)),
        name="route",
    )(logits_t, bias.reshape(N_E, 1), tri)


def _row_copy(src_ref, src_row, dst_ref, dst_row, sem):
    return pltpu.make_async_copy(src_ref.at[src_row], dst_ref.at[dst_row], sem)


def _dispatch_kernel(tok_ref, h_ref, o_ref, buf, sem, *, tg):
    i = pl.program_id(0)

    def start_block(blk, slot):
        base = blk * tg

        def issue(g, c):
            for u in range(ROW_DMA_UNROLL):
                r = g * ROW_DMA_UNROLL + u
                _row_copy(h_ref, tok_ref[base + r], buf.at[slot], r, sem.at[slot]).start(priority=u % 2)
            return c

        lax.fori_loop(0, tg // ROW_DMA_UNROLL, issue, 0)

    @pl.when(i == 0)
    def _():
        start_block(0, 0)

    @pl.when(i + 1 < pl.num_programs(0))
    def _():
        start_block(i + 1, (i + 1) & 1)

    slot = i & 1
    pltpu.make_async_copy(h_ref.at[pl.ds(0, tg)], buf.at[slot], sem.at[slot]).wait()
    o_ref[...] = buf[slot].reshape(tg, D)


def _dispatch_call(slot_token, h, *, tg):
    n_slots = slot_token.shape[0]
    return pl.pallas_call(
        functools.partial(_dispatch_kernel, tg=tg),
        grid_spec=pltpu.PrefetchScalarGridSpec(
            num_scalar_prefetch=1,
            grid=(n_slots // tg,),
            in_specs=[pl.BlockSpec(memory_space=pl.ANY)],
            out_specs=pl.BlockSpec((tg, D), lambda i, tok: (i, 0)),
            scratch_shapes=[pltpu.VMEM((2, tg, D // LANES, LANES), BF16), pltpu.SemaphoreType.DMA((2,))],
        ),
        out_shape=jax.ShapeDtypeStruct((n_slots, D), BF16),
        compiler_params=_cparams(("arbitrary",)),
        name="moe_dispatch",
    )(slot_token, h)


def _weight_tile_ring(sched, srcs, bufs, sem, layer, tn):
    col_ref, first_ref, tile_ref, te_ref, tc_ref, nt_ref = sched
    i = pl.program_id(0)
    t = tile_ref[i]

    def copies(tt, slot):
        col0 = pl.multiple_of(tc_ref[tt] * tn, tn)
        return [pltpu.make_async_copy(src.at[layer, te_ref[tt], :, pl.ds(col0, tn)], buf.at[slot], sem.at[slot])
                for src, buf in zip(srcs, bufs)]

    @pl.when(first_ref[i] == 1)
    def _():
        @pl.when(t == 0)
        def _():
            for c in copies(0, 0):
                c.start()

        @pl.when(t + 1 < nt_ref[0])
        def _():
            for c in copies(t + 1, (t + 1) & 1):
                c.start()

        for c in copies(t, t & 1):
            c.wait()

    return t & 1


def _gateup_kernel(col_ref, row_ref, valid_ref, first_ref, tile_ref, te_ref, tc_ref, nt_ref,
                   x_ref, wg_hbm, wu_hbm, o_ref, gbuf, ubuf, sem, *, layer, tn):
    slot = _weight_tile_ring((col_ref, first_ref, tile_ref, te_ref, tc_ref, nt_ref),
                             (wg_hbm, wu_hbm), (gbuf, ubuf), sem, layer, tn)

    @pl.when(valid_ref[pl.program_id(0)] == 1)
    def _():
        x = x_ref[...]
        g = lax.dot_general(x, gbuf[slot], (((1,), (0,)), ((), ())), preferred_element_type=F32)
        u = lax.dot_general(x, ubuf[slot], (((1,), (0,)), ((), ())), preferred_element_type=F32)
        o_ref[...] = ((g * _sigmoid(g)) * u).astype(BF16)

    @pl.when(valid_ref[pl.program_id(0)] == 0)
    def _():
        o_ref[...] = jnp.zeros_like(o_ref)


def _down_kernel(col_ref, row_ref, valid_ref, first_ref, tile_ref, te_ref, tc_ref, nt_ref,
                 h_ref, wd_hbm, o_ref, dbuf, sem, *, layer, tn):
    slot = _weight_tile_ring((col_ref, first_ref, tile_ref, te_ref, tc_ref, nt_ref),
                             (wd_hbm,), (dbuf,), sem, layer, tn)

    @pl.when(valid_ref[pl.program_id(0)] == 1)
    def _():
        y = lax.dot_general(h_ref[...], dbuf[slot], (((1,), (0,)), ((), ())), preferred_element_type=F32)
        o_ref[...] = y.reshape(o_ref.shape)

    @pl.when(valid_ref[pl.program_id(0)] == 0)
    def _():
        o_ref[...] = jnp.zeros_like(o_ref)


def _gateup_call(sched, x_sorted, wg, wu, layer, *, tm, tn):
    n_slots = x_sorted.shape[0]
    n_items = sched[0].shape[0]
    return pl.pallas_call(
        functools.partial(_gateup_kernel, layer=layer, tn=tn),
        grid_spec=pltpu.PrefetchScalarGridSpec(
            num_scalar_prefetch=len(sched),
            grid=(n_items,),
            in_specs=[pl.BlockSpec((tm, D), lambda i, col, row, *_: (row[i], 0)),
                      pl.BlockSpec(memory_space=pl.ANY), pl.BlockSpec(memory_space=pl.ANY)],
            out_specs=pl.BlockSpec((tm, tn), lambda i, col, row, *_: (row[i], col[i])),
            scratch_shapes=[pltpu.VMEM((2, D, tn), F32), pltpu.VMEM((2, D, tn), F32),
                            pltpu.SemaphoreType.DMA((2,))],
        ),
        out_shape=jax.ShapeDtypeStruct((n_slots, FF), BF16),
        compiler_params=_cparams(("arbitrary",)),
        name="moe_gateup",
    )(*sched, x_sorted, wg, wu)


def _down_call(sched, hs, wd, layer, *, tm, tn):
    n_slots = hs.shape[0]
    n_items = sched[0].shape[0]
    return pl.pallas_call(
        functools.partial(_down_kernel, layer=layer, tn=tn),
        grid_spec=pltpu.PrefetchScalarGridSpec(
            num_scalar_prefetch=len(sched),
            grid=(n_items,),
            in_specs=[pl.BlockSpec((tm, FF), lambda i, col, row, *_: (row[i], 0)),
                      pl.BlockSpec(memory_space=pl.ANY)],
            out_specs=pl.BlockSpec((tm, tn // LANES, LANES), lambda i, col, row, *_: (row[i], col[i], 0)),
            scratch_shapes=[pltpu.VMEM((2, FF, tn), F32), pltpu.SemaphoreType.DMA((2,))],
        ),
        out_shape=jax.ShapeDtypeStruct((n_slots, D // LANES, LANES), F32),
        compiler_params=_cparams(("arbitrary",)),
        name="moe_down",
    )(*sched, hs, wd)


def _moe_items(blocks_per_expert, n_tiles, n_blocks):
    bends = jnp.cumsum(blocks_per_expert)
    bstart = bends - blocks_per_expert
    total = bends[-1] * n_tiles
    step = jnp.arange(n_blocks * n_tiles, dtype=I32)
    j = jnp.minimum(step, total - 1)
    e = jnp.minimum(jnp.sum((j[:, None] >= (bends * n_tiles)[None, :]).astype(I32), axis=1), N_E - 1)
    local = j - bstart[e] * n_tiles
    nb = jnp.maximum(blocks_per_expert[e], 1)
    valid = step < total
    tail = jnp.maximum(step - total, 0)
    col = jnp.where(valid, local // nb, tail % n_tiles)
    row = jnp.where(valid, bstart[e] + local % nb, bends[-1] + tail // n_tiles)
    first = valid & (local % nb == 0)
    tile = jnp.maximum(jnp.cumsum(first.astype(I32)) - 1, 0)
    max_tiles = N_E * n_tiles
    where = jnp.where(first, tile, max_tiles)
    tile_e = jnp.zeros((max_tiles,), I32).at[where].set(e.astype(I32), mode="drop")
    tile_c = jnp.zeros((max_tiles,), I32).at[where].set(col.astype(I32), mode="drop")
    n_used = jnp.sum(first.astype(I32)).reshape(1)
    return (col.astype(I32), row.astype(I32), valid.astype(I32), first.astype(I32), tile.astype(I32),
            tile_e, tile_c, n_used)


def _combine_kernel(dest_ref, y_ref, x_ref, gm_ref, gw_ref, o_ref, buf, sem, *, tb, t_off, t_total):
    i = pl.program_id(0)

    def start_block(blk, slot):
        base = t_off + blk * tb

        def issue(g, c):
            for u in range(ROW_DMA_UNROLL // 2):
                r = g * (ROW_DMA_UNROLL // 2) + u
                for k in range(2):
                    _row_copy(y_ref, dest_ref[k * t_total + base + r], buf.at[slot, k], r,
                              sem.at[slot]).start(priority=k)
            return c

        lax.fori_loop(0, tb // (ROW_DMA_UNROLL // 2), issue, 0)

    @pl.when(i == 0)
    def _():
        start_block(0, 0)

    @pl.when(i + 1 < pl.num_programs(0))
    def _():
        start_block(i + 1, (i + 1) & 1)

    slot = i & 1
    for k in range(2):
        pltpu.make_async_copy(y_ref.at[pl.ds(0, tb)], buf.at[slot, k], sem.at[slot]).wait()
    gw = gw_ref[...]
    y0 = buf[slot, 0].reshape(tb, D)
    y1 = buf[slot, 1].reshape(tb, D)
    o_ref[...] = x_ref[...] + gm_ref[...] * (gw[:, 0:1] * y0 + gw[:, 1:2] * y1)


def _combine_call(dest_flat, y, x, gm, gw, *, tb, t_off, t_total, name):
    m = x.shape[0]
    if gm.shape[0] == 1:
        gm_spec = pl.BlockSpec((1, D), lambda i, d: (0, 0))
    else:
        gm_spec = pl.BlockSpec((tb, D), lambda i, d: (i, 0))
    return pl.pallas_call(
        functools.partial(_combine_kernel, tb=tb, t_off=t_off, t_total=t_total),
        grid_spec=pltpu.PrefetchScalarGridSpec(
            num_scalar_prefetch=1,
            grid=(m // tb,),
            in_specs=[pl.BlockSpec(memory_space=pl.ANY),
                      pl.BlockSpec((tb, D), lambda i, d: (i, 0)), gm_spec,
                      pl.BlockSpec((tb, 2), lambda i, d: (i, 0))],
            out_specs=pl.BlockSpec((tb, D), lambda i, d: (i, 0)),
            scratch_shapes=[pltpu.VMEM((2, 2, tb, D // LANES, LANES), F32), pltpu.SemaphoreType.DMA((2,))],
        ),
        out_shape=jax.ShapeDtypeStruct((m, D), F32),
        compiler_params=_cparams(("arbitrary",)),
        name=name,
    )(dest_flat, y, x, gm, gw)


def _moe(layer, h_all, logits_all, w_router_bias, wg, wu, wd, xp, xs, gmp, gms):
    t = h_all.shape[0]
    tm = MOE_TM
    eid, gw, rank, cnt = _route_call(logits_all[:, :N_E].T, w_router_bias, tb=640)
    counts = cnt[:, 0]
    blocks_per_expert = (counts + tm - 1) // tm
    pstart = (jnp.cumsum(blocks_per_expert) - blocks_per_expert) * tm
    first = jnp.sum(jnp.where(eid[None] == jnp.arange(N_E, dtype=I32)[:, None, None],
                              pstart[:, None, None], 0), axis=0)
    dest = first + rank
    n_blocks = (2 * t) // tm + N_E
    n_slots = n_blocks * tm
    tok = jnp.tile(jnp.arange(t, dtype=I32), 2)
    slot_token = jnp.zeros((n_slots,), I32).at[dest.reshape(-1)].set(tok)
    x_sorted = _dispatch_call(slot_token, h_all, tg=MOE_DISPATCH_ROWS)
    items1 = _moe_items(blocks_per_expert, FF // MOE_FF_TN, n_blocks)
    hs = _gateup_call(items1, x_sorted, wg, wu, layer, tm=tm, tn=MOE_FF_TN)
    items2 = _moe_items(blocks_per_expert, D // MOE_D_TN, n_blocks)
    y = _down_call(items2, hs, wd, layer, tm=tm, tn=MOE_D_TN)
    dest_flat = dest.reshape(-1)
    gwt = gw.T
    np_ = xp.shape[0]
    xp2 = _combine_call(dest_flat, y, xp, gmp, gwt[:np_], tb=256, t_off=0, t_total=t, name="moe_combine_p")
    xs2 = _combine_call(dest_flat, y, xs, gms, gwt[np_:], tb=128, t_off=np_, t_total=t, name="moe_combine_s")
    return xp2, xs2


def _attn_kernel(lam_ref, qt_ref, k_ref, vt_ref, g_ref, o_ref, m_scr, l_scr, acc_scr, sa_scr, sb_scr,
                 *, tq, out_scale):
    qi = pl.program_id(1)
    qt = qt_ref[0, 0]
    feat = lax.broadcasted_iota(I32, (C_VD, tq), 0)
    zero = jnp.zeros_like(qt)
    qq = jnp.concatenate([jnp.where(feat < C_HD, qt, zero), jnp.where(feat >= C_HD, qt, zero)], axis=1)
    m_scr[...] = jnp.full_like(m_scr, -jnp.inf)
    l_scr[...] = jnp.zeros_like(l_scr)
    acc_scr[...] = jnp.zeros_like(acc_scr)

    def scores(j, dst, masked):
        off = pl.multiple_of(j * tq, tq)
        st = _dot(k_ref[pl.ds(off, tq), :], qq)
        if masked:
            key = lax.broadcasted_iota(I32, (tq, 2 * tq), 0)
            col = lax.broadcasted_iota(I32, (tq, 2 * tq), 1)
            st = jnp.where(key <= jnp.where(col >= tq, col - tq, col), st, -jnp.inf)
        dst[...] = st

    def absorb(src, j):
        st = src[...]
        m_prev = m_scr[...]
        m_new = jnp.maximum(m_prev, jnp.max(st, axis=0, keepdims=True))
        alpha = jnp.exp2(m_prev - m_new)
        p = jnp.exp2(st - m_new[0:1, :])
        l_scr[...] = alpha * l_scr[...] + jnp.sum(p, axis=0, keepdims=True)
        acc_scr[...] = alpha[0:1, :] * acc_scr[...] + _dot(vt_ref[0, j], p.astype(BF16))
        m_scr[...] = m_new

    scores(qi, sa_scr, True)
    npair = lax.shift_right_logical(qi, 1)

    def pair(t, c):
        scores(2 * t, sb_scr, False)
        absorb(sa_scr, jnp.where(t == 0, qi, 2 * t - 1))
        scores(2 * t + 1, sa_scr, False)
        absorb(sb_scr, 2 * t)
        return c

    lax.fori_loop(0, npair, pair, 0)
    held = jnp.where(npair == 0, qi, 2 * npair - 1)

    @pl.when((qi & 1) == 1)
    def _():
        scores(qi - 1, sb_scr, False)
        absorb(sa_scr, held)
        absorb(sb_scr, qi - 1)

    @pl.when((qi & 1) == 0)
    def _():
        absorb(sa_scr, held)

    on = acc_scr[...] * (1.0 / l_scr[0:1, :])
    o = (on[:, 0:tq] - lam_ref[0] * on[:, tq:2 * tq]).T
    o = o * lax.rsqrt(jnp.mean(o * o, axis=-1, keepdims=True) + EPS) * g_ref[...] * out_scale
    o_ref[...] = o.astype(o_ref.dtype)


def _attn_call(lam, qt, k, vt, g, *, tq, out_scale):
    s = k.shape[0]
    return pl.pallas_call(
        functools.partial(_attn_kernel, tq=tq, out_scale=out_scale),
        grid=(C_H, s // tq),
        in_specs=[
            pl.BlockSpec(memory_space=pltpu.SMEM),
            pl.BlockSpec((1, 1, C_VD, tq), lambda h, i: (h, i, 0, 0)),
            pl.BlockSpec((s, C_VD), lambda h, i: (0, h)),
            pl.BlockSpec((1, s // tq, C_VD, tq), lambda h, i: (h, 0, 0, 0)),
            pl.BlockSpec((1, C_VD), lambda h, i: (0, 0)),
        ],
        out_specs=pl.BlockSpec((tq, C_VD), lambda h, i: (i, h)),
        out_shape=jax.ShapeDtypeStruct((s, C_H * C_VD), BF16),
        scratch_shapes=[pltpu.VMEM((SUBLANES, 2 * tq), F32), pltpu.VMEM((SUBLANES, 2 * tq), F32),
                        pltpu.VMEM((C_VD, 2 * tq), F32),
                        pltpu.VMEM((tq, 2 * tq), F32), pltpu.VMEM((tq, 2 * tq), F32)],
        compiler_params=_cparams(("parallel", "arbitrary")),
        name="diff_attn_prompt",
    )(lam, qt, k, vt, g)


def _sattn_kernel(pt_ref, lam_ref, wq_ref, bias_ref, *rest, out_scale, n_pg):
    k_refs, v_refs = rest[:n_pg], rest[n_pg:2 * n_pg]
    kn_ref, vn_ref, g_ref, o_ref, m_scr, l_scr, acc_scr = rest[2 * n_pg:]
    b = pl.program_id(0)
    p = pl.program_id(1)

    @pl.when(p == 0)
    def _():
        m_scr[...] = jnp.full_like(m_scr, -jnp.inf)
        l_scr[...] = jnp.zeros_like(l_scr)
        acc_scr[...] = jnp.zeros_like(acc_scr)

    wq = wq_ref[0]
    reps = kn_ref.shape[0] // LANES

    def update(s, vmat):
        m_prev = m_scr[...]
        m_new = jnp.maximum(m_prev, jnp.max(s, axis=1, keepdims=True))
        alpha = jnp.exp2(m_prev - m_new)
        pr = jnp.exp2(s - jnp.concatenate([m_new] * reps, axis=1))
        l_scr[...] = alpha * l_scr[...] + jnp.sum(pr, axis=1, keepdims=True)
        acc_scr[...] = alpha * acc_scr[...] + _dot(pr, vmat)
        m_scr[...] = m_new

    for i in range(n_pg):
        update(_dot_nt(wq, k_refs[i][0]) + bias_ref[...], v_refs[i][0])

    @pl.when(p == pl.num_programs(1) - 1)
    def _():
        n = kn_ref.shape[0]
        rows = 2 * DEC_S
        s = _dot_nt(wq, kn_ref[...])
        c = lax.broadcasted_iota(I32, (wq.shape[0], n), 0)
        col = lax.broadcasted_iota(I32, (wq.shape[0], n), 1)
        tok = lax.shift_right_logical(col, 4)
        ok = ((col & (C_H - 1)) == lax.shift_right_logical(c, 3)) \
            & (lax.shift_right_logical(tok, 2) == b) & ((tok & (DEC_S - 1)) <= (c & (DEC_S - 1)))
        update(jnp.where(ok, s, -jnp.inf), vn_ref[...])
        out = acc_scr[...] / l_scr[...]
        for h in range(C_H):
            blk = out[h * rows:(h + 1) * rows, :]
            o = blk[0:DEC_S, :] - lam_ref[0] * blk[DEC_S:rows, :]
            on = o * lax.rsqrt(jnp.mean(o * o, axis=-1, keepdims=True) + EPS) * g_ref[...] * out_scale
            o_ref[0, :, h * C_VD:(h + 1) * C_VD] = on


def _sattn_call(page_table, lam, wq, bias, ck, cv, kn, vn, g, *, out_scale, n_pg):
    nb, n_pages = page_table.shape
    nrow = wq.shape[1]
    prow = ck.shape[1]

    def page_spec(i):
        return pl.BlockSpec((1, prow, C_VD), lambda b, p, pt: (pt[b * n_pages + p * n_pg + i], 0, 0))

    const2 = lambda b, p, pt: (0, 0)
    return pl.pallas_call(
        functools.partial(_sattn_kernel, out_scale=out_scale, n_pg=n_pg),
        grid_spec=pltpu.PrefetchScalarGridSpec(
            num_scalar_prefetch=1,
            grid=(nb, n_pages // n_pg),
            in_specs=[
                pl.BlockSpec(memory_space=pltpu.SMEM),
                pl.BlockSpec((1, nrow, C_VD), lambda b, p, pt: (b, 0, 0)),
                pl.BlockSpec(bias.shape, const2),
                *[page_spec(i) for i in range(n_pg)],
                *[page_spec(i) for i in range(n_pg)],
                pl.BlockSpec(kn.shape, const2),
                pl.BlockSpec(vn.shape, const2),
                pl.BlockSpec((1, C_VD), const2),
            ],
            out_specs=pl.BlockSpec((1, DEC_S, C_H * C_VD), lambda b, p, pt: (b, 0, 0)),
            scratch_shapes=[pltpu.VMEM((nrow, LANES), F32), pltpu.VMEM((nrow, LANES), F32),
                            pltpu.VMEM((nrow, C_VD), F32)],
        ),
        out_shape=jax.ShapeDtypeStruct((nb, DEC_S, C_H * C_VD), F32),
        compiler_params=_cparams(("parallel", "arbitrary")),
        name="diff_attn_sample",
    )(page_table.reshape(-1), lam, wq, bias, *([ck] * n_pg), *([cv] * n_pg), kn, vn, g)


def _rope0_tables(pos):
    half = B_DK // 2
    freqs = 10000.0 ** (-jnp.arange(half, dtype=F32) / half)
    ang = pos.astype(F32)[:, None] * freqs[None, :]
    return jnp.cos(ang), jnp.sin(ang)


def _rope1_tables(pos):
    half = C_ROT // 2
    freqs = 500000.0 ** (-jnp.arange(half, dtype=F32) / half)
    ang = pos.astype(F32)[:, None] * freqs[None, :]
    cos, sin = jnp.cos(ang), jnp.sin(ang)
    n = pos.shape[0]
    pad = C_HD - C_ROT
    cos_t = jnp.concatenate([cos, cos, jnp.ones((n, pad), F32)], axis=1)
    sin_a = jnp.concatenate([-sin, jnp.zeros((n, half + pad), F32)], axis=1)
    sin_b = jnp.concatenate([jnp.zeros((n, half), F32), sin, jnp.zeros((n, pad), F32)], axis=1)
    rep = lambda t: jnp.tile(t, (1, LANES // C_HD))
    return rep(cos_t), rep(sin_a), rep(sin_b)


def _log_gamma():
    return jnp.log(1.0 - 2.0 ** (-5.0 - jnp.arange(B_H, dtype=F32)))


def _lanes_per_head(per_head):
    return jnp.repeat(per_head, B_DK, axis=1)


def kernel(x_prompt, x_sample, c_prompt, c_sample, state_ret, cache_k, cache_v, page_table, w_ada, b_ada, norm_mix, norm_ffn, w_in0, gmlp_ln_g, gmlp_ln_b, gmlp_w_s, gmlp_b_s, ret_norm_g, w_out0, w_in1, q_norm_g, k_norm_g, lambda_q1, lambda_k1, lambda_q2, lambda_k2, subln_g, w_out1, w_router, router_bias, w_gate, w_up, w_down):
    xp = x_prompt.reshape(SEQ, D)
    xs = x_sample.reshape(NS, D)
    pos_p = jnp.arange(SEQ)
    pos_s = PAST + jnp.arange(DEC_S)

    mc = 1 + DEC_B
    mc_pad = -(-mc // BF16_SUBLANES) * BF16_SUBLANES
    c_all = jnp.concatenate([c_prompt, c_sample, jnp.zeros((mc_pad - mc, D), F32)], axis=0)
    mods = _ada_call(c_all, w_ada, b_ada).reshape(2, mc_pad, 6, D)
    wr_pad = jnp.pad(w_router, ((0, 0), (0, LANES - N_E)))

    def mod(l, i):
        return mods[l, 0:1, i], jnp.repeat(mods[l, 1:mc, i], DEC_S, axis=0)

    l = 0
    (shp, shs), (scp, scs), (gtp, gts) = mod(l, 0), mod(l, 1), mod(l, 2)
    (sh2p, sh2s), (sc2p, sc2s), (gt2p, gt2s) = mod(l, 3), mod(l, 4), mod(l, 5)
    gmix = norm_mix[l].reshape(1, D)
    w_in = w_in0[0].astype(BF16)
    pp = _inproj_call(xp, gmix, scp, shp, w_in, tm=PROJ_TM, tn=1024, name="inproj0_p")
    ps = _inproj_call(xs, gmix, scs, shs, w_in, tm=NS, tn=1024, name="inproj0_s")

    log_g = _log_gamma()
    lng, lnb = gmlp_ln_g[0].reshape(1, A_W), gmlp_ln_b[0].reshape(1, A_W)
    rg = ret_norm_g[0].reshape(1, B_H * B_DK)
    ch = 128
    idx = jnp.arange(ch, dtype=F32)
    rel = idx[:, None] - idx[None, :]
    dec = jnp.where(rel[None] >= 0, jnp.exp(jnp.maximum(rel, 0.0)[None] * log_g[:, None, None]), 0.0)
    cd = _lanes_per_head(jnp.exp((idx + 1.0)[:, None] * log_g[None, :]))
    kd = _lanes_per_head(jnp.exp((ch - 1.0 - idx)[:, None] * log_g[None, :]))
    sdec = jnp.exp(ch * log_g)
    wm = jnp.where(jnp.tril(jnp.ones((ch, ch), dtype=bool))[None], gmlp_w_s[0], 0.0).astype(BF16)
    bs = jnp.repeat(gmlp_b_s[0].T, A_W // A_H, axis=1)
    cos0p, sin0p = _rope0_tables(pos_p)
    mix_p, ret_p = _mix0p_call(pp, lng, lnb, wm, bs, cos0p, sin0p, dec, cd, kd, rg, sdec)

    rows = 2 * DEC_S
    tloc = jnp.arange(rows) % DEC_S
    ws4 = gmlp_w_s[0][:, :DEC_S, :DEC_S]
    cw = []
    dm = []
    for d in range(DEC_S):
        src = tloc - d
        ok = src >= 0
        coef = jnp.where(ok[:, None], ws4[:, tloc, jnp.maximum(src, 0)].T, 0.0)
        cw.append(jnp.repeat(coef, A_W // A_H, axis=1))
        dm.append(_lanes_per_head(jnp.where(ok[:, None], jnp.exp(float(d) * log_g)[None, :], 0.0)))
    cw, dm = jnp.stack(cw), jnp.stack(dm)
    bs_s = jnp.repeat(gmlp_b_s[0][:, :DEC_S].T[tloc], A_W // A_H, axis=1)
    tl = tloc.astype(F32)
    cd_s = _lanes_per_head(jnp.exp((tl + 1.0)[:, None] * log_g[None, :]))
    kd_s = _lanes_per_head(jnp.exp((DEC_S - 1.0 - tl)[:, None] * log_g[None, :]))
    sdec_s = jnp.exp(DEC_S * log_g)
    cos0s, sin0s = _rope0_tables(pos_s)
    mix_s, gv_s, ret_s = _mix0s_call(ps, state_ret[0], lng, lnb, cw, bs_s, cos0s[tloc], sin0s[tloc],
                                     dm, cd_s, kd_s, rg, sdec_s)

    gffn = norm_ffn[l].reshape(1, D)
    w_out = w_out0[0].astype(BF16)
    xp, h2p, lgp = _outproj_call(mix_p, w_out, xp, gtp, gffn, sc2p, sh2p, wr_pad, tm=OUTPROJ_TM, name="outproj0_p")
    xs, h2s, lgs = _outproj_call(mix_s, w_out, xs, gts, gffn, sc2s, sh2s, wr_pad, tm=NS, name="outproj0_s")
    xp, xs = _moe(l, jnp.concatenate([h2p, h2s], axis=0), jnp.concatenate([lgp, lgs], axis=0), router_bias,
                  w_gate, w_up, w_down, xp, xs, gt2p, gt2s)

    l = 1
    (shp, shs), (scp, scs), (gtp, gts) = mod(l, 0), mod(l, 1), mod(l, 2)
    (sh2p, sh2s), (sc2p, sc2s), (gt2p, gt2s) = mod(l, 3), mod(l, 4), mod(l, 5)
    gmix = norm_mix[l].reshape(1, D)
    lambda_init = 0.8 - 0.6 * math.exp(-0.3 * l)
    lam = (jnp.exp(jnp.sum(lambda_q1[0] * lambda_k1[0])) - jnp.exp(jnp.sum(lambda_q2[0] * lambda_k2[0]))
           + lambda_init).reshape(1).astype(F32)
    hw = C_H * C_VD
    w_in = w_in1[0].astype(BF16)
    gq =jnp.tile(q_norm_g[0], LANES // C_HD).reshape(1, LANES)
    gk = jnp.tile(k_norm_g[0], LANES // C_HD).reshape(1, LANES)
    lane = jnp.arange(LANES)
    bd = (lane[:, None] // C_HD == lane[None, :] // C_HD).astype(BF16)
    tabs_p = _rope1_tables(pos_p)
    tabs_s = tuple(jnp.tile(t, (DEC_B, 1)) for t in _rope1_tables(pos_s))
    qscale = C_HD ** -0.5 * math.log2(math.e)

    q_p, k_p, k_pb, v_p, v_pb = _inproj1_call(xp, gmix, scp, shp, w_in, gq, gk, *tabs_p, bd, tm=PROJ1_TM,
                                              tn=1024, qscale=qscale, name="inproj1_p")
    q_s, k_s, _, v_s, _ = _inproj1_call(xs, gmix, scs, shs, w_in, gq, gk, *tabs_s, bd, tm=NS, tn=1024,
                                        qscale=qscale, name="inproj1_s")

    gsub = subln_g[0].reshape(1, C_VD)
    out_scale = 1.0 - lambda_init
    nqb = SEQ // ATT_TQ
    to_tiles = lambda t: t.reshape(nqb, ATT_TQ, C_H, C_VD).transpose(2, 0, 3, 1)
    att_p = _attn_call(lam, to_tiles(q_p), k_pb, to_tiles(v_pb), gsub, tq=ATT_TQ, out_scale=out_scale)

    q5 = q_s.astype(F32).reshape(DEC_B, DEC_S, C_H, 2, C_HD).transpose(0, 2, 3, 1, 4)
    wq_rows = (q5[:, :, :, :, None, :] * jnp.eye(2, dtype=F32)[None, None, :, None, :, None]
               ).reshape(DEC_B, C_H * 2 * DEC_S, C_VD)
    n_pool = cache_k.shape[1]
    prow = PAGE * C_H
    srow = jnp.arange(C_H * 2 * DEC_S)[:, None] // (2 * DEC_S)
    head_bias = jnp.where(jnp.arange(prow)[None, :] % C_H == srow, 0.0, -jnp.inf).astype(F32)
    att_s = _sattn_call(page_table, lam, wq_rows, head_bias,
                        cache_k.reshape(n_pool, prow, C_VD), cache_v.reshape(n_pool, prow, C_VD),
                        k_s.reshape(NS * C_H, C_VD), v_s.reshape(NS * C_H, C_VD), gsub,
                        out_scale=out_scale, n_pg=SATT_PAGES)
    att_s = att_s.reshape(NS, hw)

    gffn = norm_ffn[l].reshape(1, D)
    w_out = w_out1[0].astype(BF16)
    xp, h2p, lgp = _outproj_call(att_p, w_out, xp, gtp, gffn, sc2p, sh2p, wr_pad, tm=OUTPROJ_TM, name="outproj1_p")
    xs, h2s, lgs = _outproj_call(att_s, w_out, xs, gts, gffn, sc2s, sh2s, wr_pad, tm=NS, name="outproj1_s")
    xp, xs = _moe(l, jnp.concatenate([h2p, h2s], axis=0), jnp.concatenate([lgp, lgs], axis=0), router_bias,
                  w_gate, w_up, w_down, xp, xs, gt2p, gt2s)

    return (xp.reshape(1, SEQ, D), xs.reshape(DEC_B, DEC_S, D),
            ret_p.reshape(1, 1, B_H, B_DK, B_DK), ret_s.reshape(1, DEC_B, B_H, B_DK, B_DK),
            gv_s.reshape(1, DEC_B, DEC_S, A_W),
            k_p.reshape(1, 1, SEQ, C_H, C_VD), v_p.reshape(1, 1, SEQ, C_H, C_VD),
            k_s.reshape(1, DEC_B, DEC_S, C_H, C_VD), v_s.reshape(1, DEC_B, DEC_S, C_H, C_VD))
```

```python
import functools
import math

import jax
import jax.numpy as jnp
from jax import lax
from jax.experimental import pallas as pl
from jax.experimental.pallas import tpu as pltpu

F32 = jnp.float32
BF16 = jnp.bfloat16
I32 = jnp.int32

D = 2048
SEQ = 8192
DEC_B = 32
DEC_S = 4
NS = DEC_B * DEC_S
PAST = 8192
PAGE = 128
EPS = 1e-6
A_W = 1024
A_H = 8
B_H = 4
B_DK = 256
C_H = 16
C_HD = 64
C_VD = 128
C_ROT = 16
N_E = 16
N_G = 4
FF = 1536

LANES = 128
SUBLANES = 8
BF16_SUBLANES = 16
VMEM_LIMIT_BYTES = 56 * 1024 * 1024

PROJ_TM = 1024
PROJ1_TM = 512
OUTPROJ_TM = 256
MOE_TM = 256
MOE_FF_TN = 768
MOE_D_TN = 1024
MOE_DISPATCH_ROWS = 3 * MOE_TM
ROW_DMA_UNROLL = 8
ATT_TQ = 512
SATT_PAGES = 8


def _cparams(sem):
    return pltpu.CompilerParams(dimension_semantics=sem, vmem_limit_bytes=VMEM_LIMIT_BYTES)


def _sigmoid(x):
    return 1.0 / (1.0 + jnp.exp(-x))


def _gelu(x):
    return x * (0.5 * (1.0 + jnp.tanh(0.7978845608028654 * (x + 0.044715 * (x * x * x)))))


def _rms_mod(x, g, sc, sh):
    ms = jnp.mean(x * x, axis=-1, keepdims=True)
    return (x * lax.rsqrt(ms + EPS)) * g * (1.0 + sc) + sh


def _dot(a, b):
    return jnp.dot(a, b, preferred_element_type=F32)


def _dot_nt(a, b):
    return lax.dot_general(a, b, (((1,), (1,)), ((), ())), preferred_element_type=F32)


def _dot_tn(a, b):
    return lax.dot_general(a, b, (((0,), (0,)), ((), ())), preferred_element_type=F32)


def _split_bf16(a):
    hi = a.astype(BF16)
    return hi, (a - hi.astype(F32)).astype(BF16)


def _ada_kernel(c_ref, w_ref, b_ref, o_ref):
    c = c_ref[...]
    s_hi, s_lo = _split_bf16(c * _sigmoid(c))
    w_hi, w_lo = _split_bf16(w_ref[0])
    o_ref[0] = (_dot(s_hi, w_hi) + (_dot(s_lo, w_hi) + _dot(s_hi, w_lo))) + b_ref[0]


def _ada_call(c_all, w_ada, b_ada):
    nl, _, n = w_ada.shape
    mc = c_all.shape[0]
    tn = 1024
    return pl.pallas_call(
        _ada_kernel,
        grid=(nl, n // tn),
        in_specs=[
            pl.BlockSpec((mc, D), lambda l, j: (0, 0)),
            pl.BlockSpec((1, D, tn), lambda l, j: (l, 0, j)),
            pl.BlockSpec((1, 1, tn), lambda l, j: (l, 0, j)),
        ],
        out_specs=pl.BlockSpec((1, mc, tn), lambda l, j: (l, 0, j)),
        out_shape=jax.ShapeDtypeStruct((nl, mc, n), F32),
        compiler_params=_cparams(("parallel", "parallel")),
        name="ada",
    )(c_all, w_ada, b_ada.reshape(nl, 1, n))


def _inproj_kernel(x_ref, g_ref, sc_ref, sh_ref, w_ref, o_ref, h_scr):
    @pl.when(pl.program_id(1) == 0)
    def _():
        h_scr[...] = _rms_mod(x_ref[...], g_ref[...], sc_ref[...], sh_ref[...]).astype(BF16)

    o_ref[...] = _dot(h_scr[...], w_ref[...])


def _inproj_call(x, g, sc, sh, w, *, tm, tn, name):
    m = x.shape[0]
    n = w.shape[1]

    def rows(a):
        if a.shape[0] == 1:
            return pl.BlockSpec((1, D), lambda i, j: (0, 0))
        return pl.BlockSpec((tm, D), lambda i, j: (i, 0))

    return pl.pallas_call(
        _inproj_kernel,
        grid=(m // tm, n // tn),
        in_specs=[pl.BlockSpec((tm, D), lambda i, j: (i, 0)), pl.BlockSpec((1, D), lambda i, j: (0, 0)),
                  rows(sc), rows(sh), pl.BlockSpec((D, tn), lambda i, j: (0, j))],
        out_specs=pl.BlockSpec((tm, tn), lambda i, j: (i, j)),
        out_shape=jax.ShapeDtypeStruct((m, n), F32),
        scratch_shapes=[pltpu.VMEM((tm, D), BF16)],
        compiler_params=_cparams(("parallel", "arbitrary")),
        name=name,
    )(x, g, sc, sh, w)


def _qk_epilogue(acc, gain, cos_t, sin_a, sin_b, bd, scale, emit):
    for j in range(acc.shape[1] // LANES):
        sl = slice(j * LANES, (j + 1) * LANES)
        y = acc[:, sl]
        hi, lo = _split_bf16(y * y)
        ss = _dot(hi, bd) + _dot(lo, bd)
        yn = y * lax.rsqrt(ss * (1.0 / C_HD) + EPS) * gain
        r = yn * cos_t + pltpu.roll(yn, LANES - C_ROT // 2, 1) * sin_a + pltpu.roll(yn, C_ROT // 2, 1) * sin_b
        emit(sl, r * scale if scale != 1.0 else r)


def _inproj1_kernel(x_ref, g_ref, sc_ref, sh_ref, w_ref, gq_ref, gk_ref, cos_ref, sa_ref, sb_ref, bd_ref,
                    q_ref, kf_ref, kb_ref, vf_ref, vb_ref, h_scr, *, qscale, tiles):
    j = pl.program_id(1)

    @pl.when(j == 0)
    def _():
        h_scr[...] = _rms_mod(x_ref[...], g_ref[...], sc_ref[...], sh_ref[...]).astype(BF16)

    acc = _dot(h_scr[...], w_ref[...])
    tabs = (cos_ref[...], sa_ref[...], sb_ref[...], bd_ref[...])

    @pl.when(j < tiles)
    def _():
        def emit(sl, r):
            q_ref[:, sl] = r.astype(BF16)
        _qk_epilogue(acc, gq_ref[...], *tabs, qscale, emit)

    @pl.when((j >= tiles) & (j < 2 * tiles))
    def _():
        def emit(sl, r):
            kf_ref[:, sl] = r
            kb_ref[:, sl] = r.astype(BF16)
        _qk_epilogue(acc, gk_ref[...], *tabs, 1.0, emit)

    @pl.when(j >= 2 * tiles)
    def _():
        vf_ref[...] = acc
        vb_ref[...] = acc.astype(BF16)


def _inproj1_call(x, g, sc, sh, w, gq, gk, cos_t, sin_a, sin_b, bd, *, tm, tn, qscale, name):
    m = x.shape[0]
    width = w.shape[1] // 3
    tiles = width // tn

    def rows(a):
        if a.shape[0] == 1:
            return pl.BlockSpec((1, D), lambda i, j: (0, 0))
        return pl.BlockSpec((tm, D), lambda i, j: (i, 0))

    tab = pl.BlockSpec((tm, LANES), lambda i, j: (i, 0))
    one = pl.BlockSpec((1, LANES), lambda i, j: (0, 0))

    def out_spec(part):
        return pl.BlockSpec((tm, tn), lambda i, j: (i, jnp.clip(j - part * tiles, 0, tiles - 1)))

    f32o = jax.ShapeDtypeStruct((m, width), F32)
    bf16o = jax.ShapeDtypeStruct((m, width), BF16)
    return pl.pallas_call(
        functools.partial(_inproj1_kernel, qscale=qscale, tiles=tiles),
        grid=(m // tm, 3 * tiles),
        in_specs=[pl.BlockSpec((tm, D), lambda i, j: (i, 0)), pl.BlockSpec((1, D), lambda i, j: (0, 0)),
                  rows(sc), rows(sh), pl.BlockSpec((D, tn), lambda i, j: (0, j)),
                  one, one, tab, tab, tab, pl.BlockSpec((LANES, LANES), lambda i, j: (0, 0))],
        out_specs=[out_spec(0), out_spec(1), out_spec(1), out_spec(2), out_spec(2)],
        out_shape=[bf16o, f32o, bf16o, f32o, bf16o],
        scratch_shapes=[pltpu.VMEM((tm, D), BF16)],
        compiler_params=_cparams(("parallel", "arbitrary")),
        name=name,
    )(x, g, sc, sh, w, gq, gk, cos_t, sin_a, sin_b, bd)


def _mix0p_kernel(sdec_ref, p_ref, lng_ref, lnb_ref, wm_ref, bs_ref, cos_ref, sin_ref, dec_ref,
                  cd_ref, kd_ref, rg_ref, mix_ref, so_ref, st_scr):
    c = pl.program_id(0)

    @pl.when(c == 0)
    def _():
        st_scr[...] = jnp.zeros_like(st_scr)

    u = _gelu(p_ref[:, 0:A_W])
    v = _gelu(p_ref[:, A_W:2 * A_W])
    mu = jnp.mean(v, axis=-1, keepdims=True)
    vc = v - mu
    v = vc * lax.rsqrt(jnp.mean(vc * vc, axis=-1, keepdims=True) + EPS) * lng_ref[...] + lnb_ref[...]
    vb = v.astype(BF16)
    for hh in range(A_H):
        sl = slice(hh * LANES, (hh + 1) * LANES)
        sv = _dot(wm_ref[hh], vb[:, sl]) + bs_ref[:, sl]
        mix_ref[:, sl] = (u[:, sl] * sv).astype(BF16)

    cos, sin = cos_ref[...], sin_ref[...]
    half = B_DK // 2

    def rope(t):
        t1, t2 = t[:, :half], t[:, half:]
        return jnp.concatenate([t1 * cos - t2 * sin, t2 * cos + t1 * sin], axis=1)

    base = 2 * A_W
    hw = B_H * B_DK
    for hd in range(B_H):
        sl = slice(hd * B_DK, (hd + 1) * B_DK)
        q = rope(p_ref[:, base + hd * B_DK: base + (hd + 1) * B_DK])
        k = rope(p_ref[:, base + hw + hd * B_DK: base + hw + (hd + 1) * B_DK]) * (B_DK ** -0.5)
        vr = p_ref[:, base + 2 * hw + hd * B_DK: base + 2 * hw + (hd + 1) * B_DK]
        g = p_ref[:, base + 3 * hw + hd * B_DK: base + 3 * hw + (hd + 1) * B_DK]
        qb, kb, vrb = q.astype(BF16), k.astype(BF16), vr.astype(BF16)
        inner = _dot_nt(qb, kb) * dec_ref[hd]
        st = st_scr[hd]
        o = _dot(inner.astype(BF16), vrb) + _dot(qb, st.astype(BF16)) * cd_ref[:, sl]
        kdb = (k * kd_ref[:, sl]).astype(BF16)
        st_scr[hd] = sdec_ref[hd] * st + _dot_tn(kdb, vrb)
        on = o * lax.rsqrt(jnp.mean(o * o, axis=-1, keepdims=True) + EPS) * rg_ref[:, sl]
        mix_ref[:, A_W + hd * B_DK: A_W + (hd + 1) * B_DK] = ((g * _sigmoid(g)) * on).astype(BF16)

    @pl.when(c == pl.num_programs(0) - 1)
    def _():
        so_ref[...] = st_scr[...]


def _mix0p_call(p, lng, lnb, wm, bs, cos, sin, dec, cd, kd, rg, sdec):
    m = p.shape[0]
    ch = 128
    full = lambda shape: pl.BlockSpec(shape, lambda c: (0,) * len(shape))
    return pl.pallas_call(
        _mix0p_kernel,
        grid=(m // ch,),
        in_specs=[
            pl.BlockSpec(memory_space=pltpu.SMEM),
            pl.BlockSpec((ch, p.shape[1]), lambda c: (c, 0)),
            full((1, A_W)), full((1, A_W)),
            full((A_H, ch, ch)), full((ch, A_W)),
            pl.BlockSpec((ch, LANES), lambda c: (c, 0)),
            pl.BlockSpec((ch, LANES), lambda c: (c, 0)),
            full((B_H, ch, ch)), full((ch, B_H * B_DK)), full((ch, B_H * B_DK)),
            full((1, B_H * B_DK)),
        ],
        out_specs=[
            pl.BlockSpec((ch, D), lambda c: (c, 0)),
            full((B_H, B_DK, B_DK)),
        ],
        out_shape=[jax.ShapeDtypeStruct((m, D), BF16), jax.ShapeDtypeStruct((B_H, B_DK, B_DK), F32)],
        scratch_shapes=[pltpu.VMEM((B_H, B_DK, B_DK), F32)],
        compiler_params=_cparams(("arbitrary",)),
        name="mix0_prompt",
    )(sdec, p, lng, lnb, wm, bs, cos, sin, dec, cd, kd, rg)


def _mix0s_kernel(sdec_ref, p_ref, lng_ref, lnb_ref, cw_ref, bs_ref, cos_ref, sin_ref, dm_ref,
                  cd_ref, kd_ref, rg_ref, st_ref, mix_ref, gv_ref, so_ref):
    rows = 2 * DEC_S

    def shift_rows(t, d):
        return t if d == 0 else pltpu.roll(t, d, 0)

    u = _gelu(p_ref[:, 0:A_W])
    v = _gelu(p_ref[:, A_W:2 * A_W])
    mu = jnp.mean(v, axis=-1, keepdims=True)
    vc = v - mu
    v = vc * lax.rsqrt(jnp.mean(vc * vc, axis=-1, keepdims=True) + EPS) * lng_ref[...] + lnb_ref[...]
    gv_ref[...] = v
    sv = bs_ref[...]
    for d in range(DEC_S):
        sv = sv + cw_ref[d] * shift_rows(v, d)
    mix_ref[:, 0:A_W] = u * sv

    cos, sin = cos_ref[...], sin_ref[...]
    half = B_DK // 2

    def rope(t):
        t1, t2 = t[:, :half], t[:, half:]
        return jnp.concatenate([t1 * cos - t2 * sin, t2 * cos + t1 * sin], axis=1)

    grp = lax.shift_right_logical(lax.broadcasted_iota(I32, (rows, B_DK), 0), 2)
    base = 2 * A_W
    hw = B_H * B_DK
    for hd in range(B_H):
        sl = slice(hd * B_DK, (hd + 1) * B_DK)
        q = rope(p_ref[:, base + hd * B_DK: base + (hd + 1) * B_DK])
        k = rope(p_ref[:, base + hw + hd * B_DK: base + hw + (hd + 1) * B_DK]) * (B_DK ** -0.5)
        vr = p_ref[:, base + 2 * hw + hd * B_DK: base + 2 * hw + (hd + 1) * B_DK]
        g = p_ref[:, base + 3 * hw + hd * B_DK: base + 3 * hw + (hd + 1) * B_DK]
        o = jnp.zeros((rows, B_DK), F32)
        for d in range(DEC_S):
            s = jnp.sum(q * shift_rows(k, d), axis=-1, keepdims=True)
            o = o + (s * dm_ref[d][:, sl]) * shift_rows(vr, d)
        kd = k * kd_ref[:, sl]
        for bi in range(2):
            st = st_ref[bi, hd]
            mine = grp == bi
            o = o + jnp.where(mine, _dot(q, st) * cd_ref[:, sl], 0.0)
            so_ref[bi, hd] = sdec_ref[hd] * st + _dot_tn(jnp.where(mine, kd, 0.0), vr)
        on = o * lax.rsqrt(jnp.mean(o * o, axis=-1, keepdims=True) + EPS) * rg_ref[:, sl]
        mix_ref[:, A_W + hd * B_DK: A_W + (hd + 1) * B_DK] = (g * _sigmoid(g)) * on


def _mix0s_call(p, st_in, lng, lnb, cw, bs, cos, sin, dm, cd, kd, rg, sdec):
    rows = 2 * DEC_S
    nb = st_in.shape[0]
    full = lambda shape: pl.BlockSpec(shape, lambda c: (0,) * len(shape))
    return pl.pallas_call(
        _mix0s_kernel,
        grid=(nb // 2,),
        in_specs=[
            pl.BlockSpec(memory_space=pltpu.SMEM),
            pl.BlockSpec((rows, p.shape[1]), lambda c: (c, 0)),
            full((1, A_W)), full((1, A_W)),
            full((DEC_S, rows, A_W)), full((rows, A_W)),
            full((rows, LANES)), full((rows, LANES)),
            full((DEC_S, rows, B_H * B_DK)), full((rows, B_H * B_DK)), full((rows, B_H * B_DK)),
            full((1, B_H * B_DK)),
            pl.BlockSpec((2, B_H, B_DK, B_DK), lambda c: (c, 0, 0, 0)),
        ],
        out_specs=[
            pl.BlockSpec((rows, D), lambda c: (c, 0)),
            pl.BlockSpec((rows, A_W), lambda c: (c, 0)),
            pl.BlockSpec((2, B_H, B_DK, B_DK), lambda c: (c, 0, 0, 0)),
        ],
        out_shape=[
            jax.ShapeDtypeStruct((nb * DEC_S, D), F32),
            jax.ShapeDtypeStruct((nb * DEC_S, A_W), F32),
            jax.ShapeDtypeStruct(st_in.shape, F32),
        ],
        compiler_params=_cparams(("parallel",)),
        name="mix0_sample",
    )(sdec, p, lng, lnb, cw, bs, cos, sin, dm, cd, kd, rg, st_in)


def _outproj_kernel(*refs):
    mix_ref, w_ref, x_ref, gt_ref, g2_ref, sc_ref, sh_ref, wrh_ref, wrl_ref, xo_ref, h_ref, lg_ref = refs
    y = _dot(mix_ref[...].astype(BF16), w_ref[...])
    xn = x_ref[...] + gt_ref[...] * y
    xo_ref[...] = xn
    h = _rms_mod(xn, g2_ref[...], sc_ref[...], sh_ref[...])
    h_ref[...] = h.reshape(h_ref.shape).astype(BF16)
    hi = h.astype(BF16)
    lo = (h - hi.astype(F32)).astype(BF16)
    lg_ref[...] = _dot(hi, wrh_ref[...]) + (_dot(lo, wrh_ref[...]) + _dot(hi, wrl_ref[...]))


def _outproj_call(mix, w, x, gate, g2, sc, sh, wr, *, tm, name):
    m = x.shape[0]
    wr_hi = wr.astype(BF16)
    wr_lo = (wr - wr_hi.astype(F32)).astype(BF16)
    chunks = D // LANES

    def rows(a):
        if a.shape[0] == 1:
            return pl.BlockSpec((1, D), lambda i: (0, 0))
        return pl.BlockSpec((tm, D), lambda i: (i, 0))

    blk = pl.BlockSpec((tm, D), lambda i: (i, 0))
    in_specs = [blk, pl.BlockSpec((D, D), lambda i: (0, 0)), blk, rows(gate),
                pl.BlockSpec((1, D), lambda i: (0, 0)), rows(sc), rows(sh),
                pl.BlockSpec((D, LANES), lambda i: (0, 0)), pl.BlockSpec((D, LANES), lambda i: (0, 0))]
    return pl.pallas_call(
        _outproj_kernel,
        grid=(m // tm,),
        in_specs=in_specs,
        out_specs=[blk, pl.BlockSpec((tm, chunks, LANES), lambda i: (i, 0, 0)),
                   pl.BlockSpec((tm, LANES), lambda i: (i, 0))],
        out_shape=[jax.ShapeDtypeStruct((m, D), F32), jax.ShapeDtypeStruct((m, chunks, LANES), BF16),
                   jax.ShapeDtypeStruct((m, LANES), F32)],
        compiler_params=_cparams(("parallel",)),
        name=name,
    )(mix, w, x, gate, g2, sc, sh, wr_hi, wr_lo)


def _route_kernel(lg_ref, bias_ref, tri_ref, eid_ref, gw_ref, rank_ref, cnt_ref, carry):
    i = pl.program_id(0)

    @pl.when(i == 0)
    def _():
        carry[...] = jnp.zeros_like(carry)

    tb = lg_ref.shape[1]
    s = _sigmoid(lg_ref[...])
    sel = s + bias_ref[...]
    epg = N_E // N_G
    r_sel = [sel[e:e + 1, :] for e in range(N_E)]
    r_s = [s[e:e + 1, :] for e in range(N_E)]

    gs = []
    for g in range(N_G):
        a, b, c, d = r_sel[epg * g: epg * g + 4]
        hi01, lo01 = jnp.maximum(a, b), jnp.minimum(a, b)
        hi23, lo23 = jnp.maximum(c, d), jnp.minimum(c, d)
        top1 = jnp.maximum(hi01, hi23)
        top2 = jnp.maximum(jnp.minimum(hi01, hi23), jnp.where(hi01 >= hi23, lo01, lo23))
        gs.append(top1 + top2)
    best = gs[0]
    gidx = jnp.zeros((1, tb), I32)
    for g in range(1, N_G):
        better = gs[g] > best
        gidx = jnp.where(better, g, gidx)
        best = jnp.where(better, gs[g], best)

    def pick_group(rws, j):
        out = rws[(N_G - 1) * epg + j]
        for g in range(N_G - 2, -1, -1):
            out = jnp.where(gidx == g, rws[g * epg + j], out)
        return out

    ing = [pick_group(r_sel, j) for j in range(epg)]
    sg = [pick_group(r_s, j) for j in range(epg)]

    def argmax_first(vals):
        bv, bi = vals[0], jnp.zeros((1, tb), I32)
        for j in range(1, epg):
            better = vals[j] > bv
            bi = jnp.where(better, j, bi)
            bv = jnp.where(better, vals[j], bv)
        return bi

    i1 = argmax_first(ing)
    i2 = argmax_first([jnp.where(i1 == j, -jnp.inf, ing[j]) for j in range(epg)])

    def pick_local(vals, idx):
        out = vals[epg - 1]
        for j in range(epg - 2, -1, -1):
            out = jnp.where(idx == j, vals[j], out)
        return out

    g1, g2 = pick_local(sg, i1), pick_local(sg, i2)
    den = g1 + g2
    e1 = gidx * epg + i1
    e2 = gidx * epg + i2
    eid_ref[0:1, :] = e1
    eid_ref[1:2, :] = e2
    gw_ref[0:1, :] = g1 / den
    gw_ref[1:2, :] = g2 / den

    eio = lax.broadcasted_iota(I32, (N_E, tb), 0)
    oh1 = eio == e1
    oh2 = eio == e2
    member = jnp.where(oh1, 1.0, jnp.where(oh2, 1.0, 0.0))
    before = carry[:, 0:1] + _dot(member.astype(BF16), tri_ref[...])
    rank_ref[0:1, :] = jnp.sum(jnp.where(oh1, before, 0.0), axis=0, keepdims=True).astype(I32)
    rank_ref[1:2, :] = jnp.sum(jnp.where(oh2, before, 0.0), axis=0, keepdims=True).astype(I32)
    carry[...] = carry[...] + jnp.sum(member, axis=1, keepdims=True)
    cnt_ref[...] = carry[...].astype(I32)


def _route_call(logits_t, bias, *, tb):
    t = logits_t.shape[1]
    tri = (jnp.arange(tb)[:, None] < jnp.arange(tb)[None, :]).astype(BF16)
    tok = pl.BlockSpec((2, tb), lambda i: (0, i))
    return pl.pallas_call(
        _route_kernel,
        grid=(t // tb,),
        in_specs=[pl.BlockSpec((N_E, tb), lambda i: (0, i)), pl.BlockSpec((N_E, 1), lambda i: (0, 0)),
                  pl.BlockSpec((tb, tb), lambda i: (0, 0))],
        out_specs=[tok, tok, tok, pl.BlockSpec((N_E, LANES), lambda i: (0, 0))],
        out_shape=[jax.ShapeDtypeStruct((2, t), I32), jax.ShapeDtypeStruct((2, t), F32),
                   jax.ShapeDtypeStruct((2, t), I32), jax.ShapeDtypeStruct((N_E, LANES), I32)],
        scratch_shapes=[pltpu.VMEM((N_E, LANES), F32)],
        compiler_params=_cparams(("arbitrary",)),
        name="route",
    )(logits_t, bias.reshape(N_E, 1), tri)


def _row_copy(src_ref, src_row, dst_ref, dst_row, sem):
    return pltpu.make_async_copy(src_ref.at[src_row], dst_ref.at[dst_row], sem)


def _dispatch_kernel(tok_ref, h_ref, o_ref, buf, sem, *, tg):
    i = pl.program_id(0)

    def start_block(blk, slot):
        base = blk * tg

        def issue(g, c):
            for u in range(ROW_DMA_UNROLL):
                r = g * ROW_DMA_UNROLL + u
                _row_copy(h_ref, tok_ref[base + r], buf.at[slot], r, sem.at[slot]).start(priority=u % 2)
            return c

        lax.fori_loop(0, tg // ROW_DMA_UNROLL, issue, 0)

    @pl.when(i == 0)
    def _():
        start_block(0, 0)

    @pl.when(i + 1 < pl.num_programs(0))
    def _():
        start_block(i + 1, (i + 1) & 1)

    slot = i & 1
    pltpu.make_async_copy(h_ref.at[pl.ds(0, tg)], buf.at[slot], sem.at[slot]).wait()
    o_ref[...] = buf[slot].reshape(tg, D)


def _dispatch_call(slot_token, h, *, tg):
    n_slots = slot_token.shape[0]
    return pl.pallas_call(
        functools.partial(_dispatch_kernel, tg=tg),
        grid_spec=pltpu.PrefetchScalarGridSpec(
            num_scalar_prefetch=1,
            grid=(n_slots // tg,),
            in_specs=[pl.BlockSpec(memory_space=pl.ANY)],
            out_specs=pl.BlockSpec((tg, D), lambda i, tok: (i, 0)),
            scratch_shapes=[pltpu.VMEM((2, tg, D // LANES, LANES), BF16), pltpu.SemaphoreType.DMA((2,))],
        ),
        out_shape=jax.ShapeDtypeStruct((n_slots, D), BF16),
        compiler_params=_cparams(("arbitrary",)),
        name="moe_dispatch",
    )(slot_token, h)


def _weight_tile_ring(sched, srcs, bufs, sem, layer, tn):
    col_ref, first_ref, tile_ref, te_ref, tc_ref, nt_ref = sched
    i = pl.program_id(0)
    t = tile_ref[i]

    def copies(tt, slot):
        col0 = pl.multiple_of(tc_ref[tt] * tn, tn)
        return [pltpu.make_async_copy(src.at[layer, te_ref[tt], :, pl.ds(col0, tn)], buf.at[slot], sem.at[slot])
                for src, buf in zip(srcs, bufs)]

    @pl.when(first_ref[i] == 1)
    def _():
        @pl.when(t == 0)
        def _():
            for c in copies(0, 0):
                c.start()

        @pl.when(t + 1 < nt_ref[0])
        def _():
            for c in copies(t + 1, (t + 1) & 1):
                c.start()

        for c in copies(t, t & 1):
            c.wait()

    return t & 1


def _gateup_kernel(col_ref, row_ref, valid_ref, first_ref, tile_ref, te_ref, tc_ref, nt_ref,
                   x_ref, wg_hbm, wu_hbm, o_ref, gbuf, ubuf, sem, *, layer, tn):
    slot = _weight_tile_ring((col_ref, first_ref, tile_ref, te_ref, tc_ref, nt_ref),
                             (wg_hbm, wu_hbm), (gbuf, ubuf), sem, layer, tn)

    @pl.when(valid_ref[pl.program_id(0)] == 1)
    def _():
        x = x_ref[...]
        g = lax.dot_general(x, gbuf[slot], (((1,), (0,)), ((), ())), preferred_element_type=F32)
        u = lax.dot_general(x, ubuf[slot], (((1,), (0,)), ((), ())), preferred_element_type=F32)
        o_ref[...] = ((g * _sigmoid(g)) * u).astype(BF16)

    @pl.when(valid_ref[pl.program_id(0)] == 0)
    def _():
        o_ref[...] = jnp.zeros_like(o_ref)


def _down_kernel(col_ref, row_ref, valid_ref, first_ref, tile_ref, te_ref, tc_ref, nt_ref,
                 h_ref, wd_hbm, o_ref, dbuf, sem, *, layer, tn):
    slot = _weight_tile_ring((col_ref, first_ref, tile_ref, te_ref, tc_ref, nt_ref),
                             (wd_hbm,), (dbuf,), sem, layer, tn)

    @pl.when(valid_ref[pl.program_id(0)] == 1)
    def _():
        y = lax.dot_general(h_ref[...], dbuf[slot], (((1,), (0,)), ((), ())), preferred_element_type=F32)
        o_ref[...] = y.reshape(o_ref.shape)

    @pl.when(valid_ref[pl.program_id(0)] == 0)
    def _():
        o_ref[...] = jnp.zeros_like(o_ref)


def _gateup_call(sched, x_sorted, wg, wu, layer, *, tm, tn):
    n_slots = x_sorted.shape[0]
    n_items = sched[0].shape[0]
    return pl.pallas_call(
        functools.partial(_gateup_kernel, layer=layer, tn=tn),
        grid_spec=pltpu.PrefetchScalarGridSpec(
            num_scalar_prefetch=len(sched),
            grid=(n_items,),
            in_specs=[pl.BlockSpec((tm, D), lambda i, col, row, *_: (row[i], 0)),
                      pl.BlockSpec(memory_space=pl.ANY), pl.BlockSpec(memory_space=pl.ANY)],
            out_specs=pl.BlockSpec((tm, tn), lambda i, col, row, *_: (row[i], col[i])),
            scratch_shapes=[pltpu.VMEM((2, D, tn), F32), pltpu.VMEM((2, D, tn), F32),
                            pltpu.SemaphoreType.DMA((2,))],
        ),
        out_shape=jax.ShapeDtypeStruct((n_slots, FF), BF16),
        compiler_params=_cparams(("arbitrary",)),
        name="moe_gateup",
    )(*sched, x_sorted, wg, wu)


def _down_call(sched, hs, wd, layer, *, tm, tn):
    n_slots = hs.shape[0]
    n_items = sched[0].shape[0]
    return pl.pallas_call(
        functools.partial(_down_kernel, layer=layer, tn=tn),
        grid_spec=pltpu.PrefetchScalarGridSpec(
            num_scalar_prefetch=len(sched),
            grid=(n_items,),
            in_specs=[pl.BlockSpec((tm, FF), lambda i, col, row, *_: (row[i], 0)),
                      pl.BlockSpec(memory_space=pl.ANY)],
            out_specs=pl.BlockSpec((tm, tn // LANES, LANES), lambda i, col, row, *_: (row[i], col[i], 0)),
            scratch_shapes=[pltpu.VMEM((2, FF, tn), F32), pltpu.SemaphoreType.DMA((2,))],
        ),
        out_shape=jax.ShapeDtypeStruct((n_slots, D // LANES, LANES), F32),
        compiler_params=_cparams(("arbitrary",)),
        name="moe_down",
    )(*sched, hs, wd)


def _moe_items(blocks_per_expert, n_tiles, n_blocks):
    bends = jnp.cumsum(blocks_per_expert)
    bstart = bends - blocks_per_expert
    total = bends[-1] * n_tiles
    step = jnp.arange(n_blocks * n_tiles, dtype=I32)
    j = jnp.minimum(step, total - 1)
    e = jnp.minimum(jnp.sum((j[:, None] >= (bends * n_tiles)[None, :]).astype(I32), axis=1), N_E - 1)
    local = j - bstart[e] * n_tiles
    nb = jnp.maximum(blocks_per_expert[e], 1)
    valid = step < total
    tail = jnp.maximum(step - total, 0)
    col = jnp.where(valid, local // nb, tail % n_tiles)
    row = jnp.where(valid, bstart[e] + local % nb, bends[-1] + tail // n_tiles)
    first = valid & (local % nb == 0)
    tile = jnp.maximum(jnp.cumsum(first.astype(I32)) - 1, 0)
    max_tiles = N_E * n_tiles
    where = jnp.where(first, tile, max_tiles)
    tile_e = jnp.zeros((max_tiles,), I32).at[where].set(e.astype(I32), mode="drop")
    tile_c = jnp.zeros((max_tiles,), I32).at[where].set(col.astype(I32), mode="drop")
    n_used = jnp.sum(first.astype(I32)).reshape(1)
    return (col.astype(I32), row.astype(I32), valid.astype(I32), first.astype(I32), tile.astype(I32),
            tile_e, tile_c, n_used)


def _combine_kernel(dest_ref, y_ref, x_ref, gm_ref, gw_ref, o_ref, buf, sem, *, tb, t_off, t_total):
    i = pl.program_id(0)

    def start_block(blk, slot):
        base = t_off + blk * tb

        def issue(g, c):
            for u in range(ROW_DMA_UNROLL // 2):
                r = g * (ROW_DMA_UNROLL // 2) + u
                for k in range(2):
                    _row_copy(y_ref, dest_ref[k * t_total + base + r], buf.at[slot, k], r,
                              sem.at[slot]).start(priority=k)
            return c

        lax.fori_loop(0, tb // (ROW_DMA_UNROLL // 2), issue, 0)

    @pl.when(i == 0)
    def _():
        start_block(0, 0)

    @pl.when(i + 1 < pl.num_programs(0))
    def _():
        start_block(i + 1, (i + 1) & 1)

    slot = i & 1
    for k in range(2):
        pltpu.make_async_copy(y_ref.at[pl.ds(0, tb)], buf.at[slot, k], sem.at[slot]).wait()
    gw = gw_ref[...]
    y0 = buf[slot, 0].reshape(tb, D)
    y1 = buf[slot, 1].reshape(tb, D)
    o_ref[...] = x_ref[...] + gm_ref[...] * (gw[:, 0:1] * y0 + gw[:, 1:2] * y1)


def _combine_call(dest_flat, y, x, gm, gw, *, tb, t_off, t_total, name):
    m = x.shape[0]
    if gm.shape[0] == 1:
        gm_spec = pl.BlockSpec((1, D), lambda i, d: (0, 0))
    else:
        gm_spec = pl.BlockSpec((tb, D), lambda i, d: (i, 0))
    return pl.pallas_call(
        functools.partial(_combine_kernel, tb=tb, t_off=t_off, t_total=t_total),
        grid_spec=pltpu.PrefetchScalarGridSpec(
            num_scalar_prefetch=1,
            grid=(m // tb,),
            in_specs=[pl.BlockSpec(memory_space=pl.ANY),
                      pl.BlockSpec((tb, D), lambda i, d: (i, 0)), gm_spec,
                      pl.BlockSpec((tb, 2), lambda i, d: (i, 0))],
            out_specs=pl.BlockSpec((tb, D), lambda i, d: (i, 0)),
            scratch_shapes=[pltpu.VMEM((2, 2, tb, D // LANES, LANES), F32), pltpu.SemaphoreType.DMA((2,))],
        ),
        out_shape=jax.ShapeDtypeStruct((m, D), F32),
        compiler_params=_cparams(("arbitrary",)),
        name=name,
    )(dest_flat, y, x, gm, gw)


def _moe(layer, h_all, logits_all, w_router_bias, wg, wu, wd, xp, xs, gmp, gms):
    t = h_all.shape[0]
    tm = MOE_TM
    eid, gw, rank, cnt = _route_call(logits_all[:, :N_E].T, w_router_bias, tb=640)
    counts = cnt[:, 0]
    blocks_per_expert = (counts + tm - 1) // tm
    pstart = (jnp.cumsum(blocks_per_expert) - blocks_per_expert) * tm
    first = jnp.sum(jnp.where(eid[None] == jnp.arange(N_E, dtype=I32)[:, None, None],
                              pstart[:, None, None], 0), axis=0)
    dest = first + rank
    n_blocks = (2 * t) // tm + N_E
    n_slots = n_blocks * tm
    tok = jnp.tile(jnp.arange(t, dtype=I32), 2)
    slot_token = jnp.zeros((n_slots,), I32).at[dest.reshape(-1)].set(tok)
    x_sorted = _dispatch_call(slot_token, h_all, tg=MOE_DISPATCH_ROWS)
    items1 = _moe_items(blocks_per_expert, FF // MOE_FF_TN, n_blocks)
    hs = _gateup_call(items1, x_sorted, wg, wu, layer, tm=tm, tn=MOE_FF_TN)
    items2 = _moe_items(blocks_per_expert, D // MOE_D_TN, n_blocks)
    y = _down_call(items2, hs, wd, layer, tm=tm, tn=MOE_D_TN)
    dest_flat = dest.reshape(-1)
    gwt = gw.T
    np_ = xp.shape[0]
    xp2 = _combine_call(dest_flat, y, xp, gmp, gwt[:np_], tb=256, t_off=0, t_total=t, name="moe_combine_p")
    xs2 = _combine_call(dest_flat, y, xs, gms, gwt[np_:], tb=128, t_off=np_, t_total=t, name="moe_combine_s")
    return xp2, xs2


def _attn_kernel(lam_ref, qt_ref, k_ref, vt_ref, g_ref, o_ref, m_scr, l_scr, acc_scr, sa_scr, sb_scr,
                 *, tq, out_scale):
    qi = pl.program_id(1)
    qt = qt_ref[0, 0]
    feat = lax.broadcasted_iota(I32, (C_VD, tq), 0)
    zero = jnp.zeros_like(qt)
    qq = jnp.concatenate([jnp.where(feat < C_HD, qt, zero), jnp.where(feat >= C_HD, qt, zero)], axis=1)
    m_scr[...] = jnp.full_like(m_scr, -jnp.inf)
    l_scr[...] = jnp.zeros_like(l_scr)
    acc_scr[...] = jnp.zeros_like(acc_scr)

    def scores(j, dst, masked):
        off = pl.multiple_of(j * tq, tq)
        st = _dot(k_ref[pl.ds(off, tq), :], qq)
        if masked:
            key = lax.broadcasted_iota(I32, (tq, 2 * tq), 0)
            col = lax.broadcasted_iota(I32, (tq, 2 * tq), 1)
            st = jnp.where(key <= jnp.where(col >= tq, col - tq, col), st, -jnp.inf)
        dst[...] = st

    def absorb(src, j):
        st = src[...]
        m_prev = m_scr[...]
        m_new = jnp.maximum(m_prev, jnp.max(st, axis=0, keepdims=True))
        alpha = jnp.exp2(m_prev - m_new)
        p = jnp.exp2(st - m_new[0:1, :])
        l_scr[...] = alpha * l_scr[...] + jnp.sum(p, axis=0, keepdims=True)
        acc_scr[...] = alpha[0:1, :] * acc_scr[...] + _dot(vt_ref[0, j], p.astype(BF16))
        m_scr[...] = m_new

    scores(qi, sa_scr, True)
    npair = lax.shift_right_logical(qi, 1)

    def pair(t, c):
        scores(2 * t, sb_scr, False)
        absorb(sa_scr, jnp.where(t == 0, qi, 2 * t - 1))
        scores(2 * t + 1, sa_scr, False)
        absorb(sb_scr, 2 * t)
        return c

    lax.fori_loop(0, npair, pair, 0)
    held = jnp.where(npair == 0, qi, 2 * npair - 1)

    @pl.when((qi & 1) == 1)
    def _():
        scores(qi - 1, sb_scr, False)
        absorb(sa_scr, held)
        absorb(sb_scr, qi - 1)

    @pl.when((qi & 1) == 0)
    def _():
        absorb(sa_scr, held)

    on = acc_scr[...] * (1.0 / l_scr[0:1, :])
    o = (on[:, 0:tq] - lam_ref[0] * on[:, tq:2 * tq]).T
    o = o * lax.rsqrt(jnp.mean(o * o, axis=-1, keepdims=True) + EPS) * g_ref[...] * out_scale
    o_ref[...] = o.astype(o_ref.dtype)


def _attn_call(lam, qt, k, vt, g, *, tq, out_scale):
    s = k.shape[0]
    return pl.pallas_call(
        functools.partial(_attn_kernel, tq=tq, out_scale=out_scale),
        grid=(C_H, s // tq),
        in_specs=[
            pl.BlockSpec(memory_space=pltpu.SMEM),
            pl.BlockSpec((1, 1, C_VD, tq), lambda h, i: (h, i, 0, 0)),
            pl.BlockSpec((s, C_VD), lambda h, i: (0, h)),
            pl.BlockSpec((1, s // tq, C_VD, tq), lambda h, i: (h, 0, 0, 0)),
            pl.BlockSpec((1, C_VD), lambda h, i: (0, 0)),
        ],
        out_specs=pl.BlockSpec((tq, C_VD), lambda h, i: (i, h)),
        out_shape=jax.ShapeDtypeStruct((s, C_H * C_VD), BF16),
        scratch_shapes=[pltpu.VMEM((SUBLANES, 2 * tq), F32), pltpu.VMEM((SUBLANES, 2 * tq), F32),
                        pltpu.VMEM((C_VD, 2 * tq), F32),
                        pltpu.VMEM((tq, 2 * tq), F32), pltpu.VMEM((tq, 2 * tq), F32)],
        compiler_params=_cparams(("parallel", "arbitrary")),
        name="diff_attn_prompt",
    )(lam, qt, k, vt, g)


def _sattn_kernel(pt_ref, lam_ref, wq_ref, bias_ref, *rest, out_scale, n_pg):
    k_refs, v_refs = rest[:n_pg], rest[n_pg:2 * n_pg]
    kn_ref, vn_ref, g_ref, o_ref, m_scr, l_scr, acc_scr = rest[2 * n_pg:]
    b = pl.program_id(0)
    p = pl.program_id(1)

    @pl.when(p == 0)
    def _():
        m_scr[...] = jnp.full_like(m_scr, -jnp.inf)
        l_scr[...] = jnp.zeros_like(l_scr)
        acc_scr[...] = jnp.zeros_like(acc_scr)

    wq = wq_ref[0]
    reps = kn_ref.shape[0] // LANES

    def update(s, vmat):
        m_prev = m_scr[...]
        m_new = jnp.maximum(m_prev, jnp.max(s, axis=1, keepdims=True))
        alpha = jnp.exp2(m_prev - m_new)
        pr = jnp.exp2(s - jnp.concatenate([m_new] * reps, axis=1))
        l_scr[...] = alpha * l_scr[...] + jnp.sum(pr, axis=1, keepdims=True)
        acc_scr[...] = alpha * acc_scr[...] + _dot(pr, vmat)
        m_scr[...] = m_new

    for i in range(n_pg):
        update(_dot_nt(wq, k_refs[i][0]) + bias_ref[...], v_refs[i][0])

    @pl.when(p == pl.num_programs(1) - 1)
    def _():
        n = kn_ref.shape[0]
        rows = 2 * DEC_S
        s = _dot_nt(wq, kn_ref[...])
        c = lax.broadcasted_iota(I32, (wq.shape[0], n), 0)
        col = lax.broadcasted_iota(I32, (wq.shape[0], n), 1)
        tok = lax.shift_right_logical(col, 4)
        ok = ((col & (C_H - 1)) == lax.shift_right_logical(c, 3)) \
            & (lax.shift_right_logical(tok, 2) == b) & ((tok & (DEC_S - 1)) <= (c & (DEC_S - 1)))
        update(jnp.where(ok, s, -jnp.inf), vn_ref[...])
        out = acc_scr[...] / l_scr[...]
        for h in range(C_H):
            blk = out[h * rows:(h + 1) * rows, :]
            o = blk[0:DEC_S, :] - lam_ref[0] * blk[DEC_S:rows, :]
            on = o * lax.rsqrt(jnp.mean(o * o, axis=-1, keepdims=True) + EPS) * g_ref[...] * out_scale
            o_ref[0, :, h * C_VD:(h + 1) * C_VD] = on


def _sattn_call(page_table, lam, wq, bias, ck, cv, kn, vn, g, *, out_scale, n_pg):
    nb, n_pages = page_table.shape
    nrow = wq.shape[1]
    prow = ck.shape[1]

    def page_spec(i):
        return pl.BlockSpec((1, prow, C_VD), lambda b, p, pt: (pt[b * n_pages + p * n_pg + i], 0, 0))

    const2 = lambda b, p, pt: (0, 0)
    return pl.pallas_call(
        functools.partial(_sattn_kernel, out_scale=out_scale, n_pg=n_pg),
        grid_spec=pltpu.PrefetchScalarGridSpec(
            num_scalar_prefetch=1,
            grid=(nb, n_pages // n_pg),
            in_specs=[
                pl.BlockSpec(memory_space=pltpu.SMEM),
                pl.BlockSpec((1, nrow, C_VD), lambda b, p, pt: (b, 0, 0)),
                pl.BlockSpec(bias.shape, const2),
                *[page_spec(i) for i in range(n_pg)],
                *[page_spec(i) for i in range(n_pg)],
                pl.BlockSpec(kn.shape, const2),
                pl.BlockSpec(vn.shape, const2),
                pl.BlockSpec((1, C_VD), const2),
            ],
            out_specs=pl.BlockSpec((1, DEC_S, C_H * C_VD), lambda b, p, pt: (b, 0, 0)),
            scratch_shapes=[pltpu.VMEM((nrow, LANES), F32), pltpu.VMEM((nrow, LANES), F32),
                            pltpu.VMEM((nrow, C_VD), F32)],
        ),
        out_shape=jax.ShapeDtypeStruct((nb, DEC_S, C_H * C_VD), F32),
        compiler_params=_cparams(("parallel", "arbitrary")),
        name="diff_attn_sample",
    )(page_table.reshape(-1), lam, wq, bias, *([ck] * n_pg), *([cv] * n_pg), kn, vn, g)


def _rope0_tables(pos):
    half = B_DK // 2
    freqs = 10000.0 ** (-jnp.arange(half, dtype=F32) / half)
    ang = pos.astype(F32)[:, None] * freqs[None, :]
    return jnp.cos(ang), jnp.sin(ang)


def _rope1_tables(pos):
    half = C_ROT // 2
    freqs = 500000.0 ** (-jnp.arange(half, dtype=F32) / half)
    ang = pos.astype(F32)[:, None] * freqs[None, :]
    cos, sin = jnp.cos(ang), jnp.sin(ang)
    n = pos.shape[0]
    pad = C_HD - C_ROT
    cos_t = jnp.concatenate([cos, cos, jnp.ones((n, pad), F32)], axis=1)
    sin_a = jnp.concatenate([-sin, jnp.zeros((n, half + pad), F32)], axis=1)
    sin_b = jnp.concatenate([jnp.zeros((n, half), F32), sin, jnp.zeros((n, pad), F32)], axis=1)
    rep = lambda t: jnp.tile(t, (1, LANES // C_HD))
    return rep(cos_t), rep(sin_a), rep(sin_b)


def _log_gamma():
    return jnp.log(1.0 - 2.0 ** (-5.0 - jnp.arange(B_H, dtype=F32)))


def _lanes_per_head(per_head):
    return jnp.repeat(per_head, B_DK, axis=1)


def kernel(x_prompt, x_sample, c_prompt, c_sample, state_ret, cache_k, cache_v, page_table, w_ada, b_ada, norm_mix, norm_ffn, w_in0, gmlp_ln_g, gmlp_ln_b, gmlp_w_s, gmlp_b_s, ret_norm_g, w_out0, w_in1, q_norm_g, k_norm_g, lambda_q1, lambda_k1, lambda_q2, lambda_k2, subln_g, w_out1, w_router, router_bias, w_gate, w_up, w_down):
    xp = x_prompt.reshape(SEQ, D)
    xs = x_sample.reshape(NS, D)
    pos_p = jnp.arange(SEQ)
    pos_s = PAST + jnp.arange(DEC_S)

    mc = 1 + DEC_B
    mc_pad = -(-mc // BF16_SUBLANES) * BF16_SUBLANES
    c_all = jnp.concatenate([c_prompt, c_sample, jnp.zeros((mc_pad - mc, D), F32)], axis=0)
    mods = _ada_call(c_all, w_ada, b_ada).reshape(2, mc_pad, 6, D)
    wr_pad = jnp.pad(w_router, ((0, 0), (0, LANES - N_E)))

    def mod(l, i):
        return mods[l, 0:1, i], jnp.repeat(mods[l, 1:mc, i], DEC_S, axis=0)

    l = 0
    (shp, shs), (scp, scs), (gtp, gts) = mod(l, 0), mod(l, 1), mod(l, 2)
    (sh2p, sh2s), (sc2p, sc2s), (gt2p, gt2s) = mod(l, 3), mod(l, 4), mod(l, 5)
    gmix = norm_mix[l].reshape(1, D)
    w_in = w_in0[0].astype(BF16)
    pp = _inproj_call(xp, gmix, scp, shp, w_in, tm=PROJ_TM, tn=1024, name="inproj0_p")
    ps = _inproj_call(xs, gmix, scs, shs, w_in, tm=NS, tn=1024, name="inproj0_s")

    log_g = _log_gamma()
    lng, lnb = gmlp_ln_g[0].reshape(1, A_W), gmlp_ln_b[0].reshape(1, A_W)
    rg = ret_norm_g[0].reshape(1, B_H * B_DK)
    ch = 128
    idx = jnp.arange(ch, dtype=F32)
    rel = idx[:, None] - idx[None, :]
    dec = jnp.where(rel[None] >= 0, jnp.exp(jnp.maximum(rel, 0.0)[None] * log_g[:, None, None]), 0.0)
    cd = _lanes_per_head(jnp.exp((idx + 1.0)[:, None] * log_g[None, :]))
    kd = _lanes_per_head(jnp.exp((ch - 1.0 - idx)[:, None] * log_g[None, :]))
    sdec = jnp.exp(ch * log_g)
    wm = jnp.where(jnp.tril(jnp.ones((ch, ch), dtype=bool))[None], gmlp_w_s[0], 0.0).astype(BF16)
    bs = jnp.repeat(gmlp_b_s[0].T, A_W // A_H, axis=1)
    cos0p, sin0p = _rope0_tables(pos_p)
    mix_p, ret_p = _mix0p_call(pp, lng, lnb, wm, bs, cos0p, sin0p, dec, cd, kd, rg, sdec)

    rows = 2 * DEC_S
    tloc = jnp.arange(rows) % DEC_S
    ws4 = gmlp_w_s[0][:, :DEC_S, :DEC_S]
    cw = []
    dm = []
    for d in range(DEC_S):
        src = tloc - d
        ok = src >= 0
        coef = jnp.where(ok[:, None], ws4[:, tloc, jnp.maximum(src, 0)].T, 0.0)
        cw.append(jnp.repeat(coef, A_W // A_H, axis=1))
        dm.append(_lanes_per_head(jnp.where(ok[:, None], jnp.exp(float(d) * log_g)[None, :], 0.0)))
    cw, dm = jnp.stack(cw), jnp.stack(dm)
    bs_s = jnp.repeat(gmlp_b_s[0][:, :DEC_S].T[tloc], A_W // A_H, axis=1)
    tl = tloc.astype(F32)
    cd_s = _lanes_per_head(jnp.exp((tl + 1.0)[:, None] * log_g[None, :]))
    kd_s = _lanes_per_head(jnp.exp((DEC_S - 1.0 - tl)[:, None] * log_g[None, :]))
    sdec_s = jnp.exp(DEC_S * log_g)
    cos0s, sin0s = _rope0_tables(pos_s)
    mix_s, gv_s, ret_s = _mix0s_call(ps, state_ret[0], lng, lnb, cw, bs_s, cos0s[tloc], sin0s[tloc],
                                     dm, cd_s, kd_s, rg, sdec_s)

    gffn = norm_ffn[l].reshape(1, D)
    w_out = w_out0[0].astype(BF16)
    xp, h2p, lgp = _outproj_call(mix_p, w_out, xp, gtp, gffn, sc2p, sh2p, wr_pad, tm=OUTPROJ_TM, name="outproj0_p")
    xs, h2s, lgs = _outproj_call(mix_s, w_out, xs, gts, gffn, sc2s, sh2s, wr_pad, tm=NS, name="outproj0_s")
    xp, xs = _moe(l, jnp.concatenate([h2p, h2s], axis=0), jnp.concatenate([lgp, lgs], axis=0), router_bias,
                  w_gate, w_up, w_down, xp, xs, gt2p, gt2s)

    l = 1
    (shp, shs), (scp, scs), (gtp, gts) = mod(l, 0), mod(l, 1), mod(l, 2)
    (sh2p, sh2s), (sc2p, sc2s), (gt2p, gt2s) = mod(l, 3), mod(l, 4), mod(l, 5)
    gmix = norm_mix[l].reshape(1, D)
    lambda_init = 0.8 - 0.6 * math.exp(-0.3 * l)
    lam = (jnp.exp(jnp.sum(lambda_q1[0] * lambda_k1[0])) - jnp.exp(jnp.sum(lambda_q2[0] * lambda_k2[0]))
           + lambda_init).reshape(1).astype(F32)
    hw = C_H * C_VD
    w_in = w_in1[0].astype(BF16)
    gq =jnp.tile(q_norm_g[0], LANES // C_HD).reshape(1, LANES)
    gk = jnp.tile(k_norm_g[0], LANES // C_HD).reshape(1, LANES)
    lane = jnp.arange(LANES)
    bd = (lane[:, None] // C_HD == lane[None, :] // C_HD).astype(BF16)
    tabs_p = _rope1_tables(pos_p)
    tabs_s = tuple(jnp.tile(t, (DEC_B, 1)) for t in _rope1_tables(pos_s))
    qscale = C_HD ** -0.5 * math.log2(math.e)

    q_p, k_p, k_pb, v_p, v_pb = _inproj1_call(xp, gmix, scp, shp, w_in, gq, gk, *tabs_p, bd, tm=PROJ1_TM,
                                              tn=1024, qscale=qscale, name="inproj1_p")
    q_s, k_s, _, v_s, _ = _inproj1_call(xs, gmix, scs, shs, w_in, gq, gk, *tabs_s, bd, tm=NS, tn=1024,
                                        qscale=qscale, name="inproj1_s")

    gsub = subln_g[0].reshape(1, C_VD)
    out_scale = 1.0 - lambda_init
    nqb = SEQ // ATT_TQ
    to_tiles = lambda t: t.reshape(nqb, ATT_TQ, C_H, C_VD).transpose(2, 0, 3, 1)
    att_p = _attn_call(lam, to_tiles(q_p), k_pb, to_tiles(v_pb), gsub, tq=ATT_TQ, out_scale=out_scale)

    q5 = q_s.astype(F32).reshape(DEC_B, DEC_S, C_H, 2, C_HD).transpose(0, 2, 3, 1, 4)
    wq_rows = (q5[:, :, :, :, None, :] * jnp.eye(2, dtype=F32)[None, None, :, None, :, None]
               ).reshape(DEC_B, C_H * 2 * DEC_S, C_VD)
    n_pool = cache_k.shape[1]
    prow = PAGE * C_H
    srow = jnp.arange(C_H * 2 * DEC_S)[:, None] // (2 * DEC_S)
    head_bias = jnp.where(jnp.arange(prow)[None, :] % C_H == srow, 0.0, -jnp.inf).astype(F32)
    att_s = _sattn_call(page_table, lam, wq_rows, head_bias,
                        cache_k.reshape(n_pool, prow, C_VD), cache_v.reshape(n_pool, prow, C_VD),
                        k_s.reshape(NS * C_H, C_VD), v_s.reshape(NS * C_H, C_VD), gsub,
                        out_scale=out_scale, n_pg=SATT_PAGES)
    att_s = att_s.reshape(NS, hw)

    gffn = norm_ffn[l].reshape(1, D)
    w_out = w_out1[0].astype(BF16)
    xp, h2p, lgp = _outproj_call(att_p, w_out, xp, gtp, gffn, sc2p, sh2p, wr_pad, tm=OUTPROJ_TM, name="outproj1_p")
    xs, h2s, lgs = _outproj_call(att_s, w_out, xs, gts, gffn, sc2s, sh2s, wr_pad, tm=NS, name="outproj1_s")
    xp, xs = _moe(l, jnp.concatenate([h2p, h2s], axis=0), jnp.concatenate([lgp, lgs], axis=0), router_bias,
                  w_gate, w_up, w_down, xp, xs, gt2p, gt2s)

    return (xp.reshape(1, SEQ, D), xs.reshape(DEC_B, DEC_S, D),
            ret_p.reshape(1, 1, B_H, B_DK, B_DK), ret_s.reshape(1, DEC_B, B_H, B_DK, B_DK),
            gv_s.reshape(1, DEC_B, DEC_S, A_W),
            k_p.reshape(1, 1, SEQ, C_H, C_VD), v_p.reshape(1, 1, SEQ, C_H, C_VD),
            k_s.reshape(1, DEC_B, DEC_S, C_H, C_VD), v_s.reshape(1, DEC_B, DEC_S, C_H, C_VD))
```
